```python
import math
import jax, jax.numpy as jnp
from jax import lax
import numpy as np

D_MODEL = 1024
BATCH = 2
SEQ = 8192
DEPTH = 1

EXPAND = 2
D_INNER = EXPAND * D_MODEL
D_CONV = D_INNER // 2
D_RET = D_INNER - D_CONV
RET_HEADS = 4
RET_QK_DIM = 256
RET_V_DIM = D_RET // RET_HEADS
CONV_WIDTH = 31
CHUNK = 128
ROPE_BASE = 10000.0
NORM_EPS = 1e-6
N_IN = 3 * D_CONV + 2 * RET_HEADS * RET_QK_DIM + 2 * D_RET

kernel_name = "hybrid_conformer_conv_retention_adaln"


def _rmsnorm(x, g):
    x32 = x.astype(jnp.float32)
    y = x32 * lax.rsqrt(jnp.mean(x32 * x32, axis=-1, keepdims=True) + NORM_EPS)
    return (y * g.astype(jnp.float32)).astype(x.dtype)


def _layernorm(x, g, b):
    x32 = x.astype(jnp.float32)
    mu = jnp.mean(x32, axis=-1, keepdims=True)
    var = jnp.mean(jnp.square(x32 - mu), axis=-1, keepdims=True)
    y = (x32 - mu) * lax.rsqrt(var + NORM_EPS)
    return (y * g.astype(jnp.float32) + b.astype(jnp.float32)).astype(x.dtype)


def _rotary(x):
    s, dh = x.shape[1], x.shape[-1]
    inv_freq = 1.0 / (ROPE_BASE ** jnp.linspace(0.0, 1.0, dh // 2, dtype=jnp.float32))
    pos = jnp.arange(s, dtype=jnp.float32)
    theta = pos[:, None] * inv_freq[None, :]
    cos = jnp.cos(theta)[None, :, None, :]
    sin = jnp.sin(theta)[None, :, None, :]
    x32 = x.astype(jnp.float32)
    x1, x2 = x32[..., : dh // 2], x32[..., dh // 2:]
    out = jnp.concatenate([x1 * cos - x2 * sin, x1 * sin + x2 * cos], axis=-1)
    return out.astype(x.dtype)


def _causal_depthwise_conv(u, w, b):
    c = u.shape[-1]
    kern = w.astype(u.dtype)[:, None, :]
    y = lax.conv_general_dilated(
        u, kern, window_strides=(1,), padding=[(CONV_WIDTH - 1, 0)],
        dimension_numbers=("NWC", "WIO", "NWC"), feature_group_count=c)
    return y + b.astype(u.dtype)


def _retention_chunkwise(q, k, v):
    bsz, s, h, dk = q.shape
    dv = v.shape[-1]
    n = s // CHUNK

    def to_chunks(t):
        d = t.shape[-1]
        return t.astype(jnp.float32).reshape(bsz, n, CHUNK, h, d).transpose(1, 0, 3, 2, 4)

    qc, kc, vc = to_chunks(q), to_chunks(k), to_chunks(v)

    log_g = jnp.log(1.0 - jnp.exp2(-5.0 - jnp.arange(h, dtype=jnp.float32)))
    idx = jnp.arange(CHUNK, dtype=jnp.float32)
    diff = idx[:, None] - idx[None, :]
    decay_mask = jnp.where(diff >= 0,
                           jnp.exp(log_g[:, None, None] * jnp.maximum(diff, 0.0)[None]),
                           0.0)
    query_decay = jnp.exp(log_g[:, None] * (idx + 1.0)[None, :])
    key_decay = jnp.exp(log_g[:, None] * (CHUNK - 1.0 - idx)[None, :])
    chunk_decay = jnp.exp(log_g * CHUNK)

    def step(state, inp):
        q_i, k_i, v_i = inp
        scores = jnp.einsum("bhid,bhjd->bhij", q_i, k_i) * decay_mask[None]
        inner = jnp.einsum("bhij,bhjv->bhiv", scores, v_i)
        cross = jnp.einsum("bhid,bhdv->bhiv", q_i, state) * query_decay[None, :, :, None]
        new_state = state * chunk_decay[None, :, None, None] + jnp.einsum(
            "bhjd,bhjv->bhdv", k_i * key_decay[None, :, :, None], v_i)
        return new_state, inner + cross

    state0 = jnp.zeros((bsz, h, dk, dv), jnp.float32)
    _, out = lax.scan(step, state0, (qc, kc, vc))
    return out.transpose(1, 0, 3, 2, 4).reshape(bsz, s, h, dv)


def _head_groupnorm(o, g, b):
    mu = jnp.mean(o, axis=-1, keepdims=True)
    var = jnp.mean(jnp.square(o - mu), axis=-1, keepdims=True)
    y = ((o - mu) * lax.rsqrt(var + NORM_EPS)).reshape(o.shape[0], o.shape[1], -1)
    return y * g.astype(jnp.float32) + b.astype(jnp.float32)


def setup_inputs(seed: int = 0) -> dict:
    key = jax.random.key(seed)
    ks = jax.random.split(key, 16)
    f32 = jnp.float32
    nrm = lambda k, shape, s: jax.random.normal(k, shape, f32) * s
    return {
        "x": nrm(ks[0], (BATCH, SEQ, D_MODEL), 1.0),
        "c": nrm(ks[1], (BATCH, D_MODEL), 1.0),
        "ada_w": nrm(ks[2], (DEPTH, D_MODEL, 3 * D_MODEL), 0.5 * D_MODEL ** -0.5),
        "ada_b": nrm(ks[3], (DEPTH, 3 * D_MODEL), 0.02),
        "norm_g": 1.0 + nrm(ks[4], (DEPTH, D_MODEL), 0.02),
        "w_in": nrm(ks[5], (DEPTH, D_MODEL, N_IN), D_MODEL ** -0.5),
        "conv_w": nrm(ks[6], (DEPTH, CONV_WIDTH, D_CONV), CONV_WIDTH ** -0.5),
        "conv_b": nrm(ks[7], (DEPTH, D_CONV), 0.02),
        "conv_ln_g": 1.0 + nrm(ks[8], (DEPTH, D_CONV), 0.02),
        "conv_ln_b": nrm(ks[9], (DEPTH, D_CONV), 0.02),
        "conv_pw": nrm(ks[10], (DEPTH, D_CONV, D_CONV), D_CONV ** -0.5),
        "ret_gn_g": 1.0 + nrm(ks[11], (DEPTH, D_RET), 0.02),
        "ret_gn_b": nrm(ks[12], (DEPTH, D_RET), 0.02),
        "w_out": nrm(ks[13], (DEPTH, D_INNER, D_MODEL), D_INNER ** -0.5),
        "final_g": 1.0 + nrm(ks[14], (D_MODEL,), 0.02),
    }


def reference(x, c, ada_w, ada_b, norm_g, w_in, conv_w, conv_b, conv_ln_g, conv_ln_b,
              conv_pw, ret_gn_g, ret_gn_b, w_out, final_g):
    bsz, s, _ = x.shape
    dt = x.dtype
    c_act = jax.nn.silu(c)
    h = x
    for l in range(DEPTH):
        mod = c_act @ ada_w[l] + ada_b[l]
        shift, scale, gate = jnp.split(mod[:, None, :], 3, axis=-1)
        u = _rmsnorm(h, norm_g[l]) * (1.0 + scale) + shift

        proj = u @ w_in[l]
        o1 = D_CONV
        o2 = o1 + D_CONV
        o3 = o2 + D_CONV
        o4 = o3 + RET_HEADS * RET_QK_DIM
        o5 = o4 + RET_HEADS * RET_QK_DIM
        o6 = o5 + D_RET
        conv_a, conv_bp, conv_gate = proj[..., :o1], proj[..., o1:o2], proj[..., o2:o3]
        q = proj[..., o3:o4].reshape(bsz, s, RET_HEADS, RET_QK_DIM)
        k = proj[..., o4:o5].reshape(bsz, s, RET_HEADS, RET_QK_DIM)
        v = proj[..., o5:o6].reshape(bsz, s, RET_HEADS, RET_V_DIM)
        ret_gate = proj[..., o6:]

        a = conv_a * jax.nn.sigmoid(conv_bp)
        a = _causal_depthwise_conv(a, conv_w[l], conv_b[l])
        a = jax.nn.silu(_layernorm(a, conv_ln_g[l], conv_ln_b[l]))
        a = a @ conv_pw[l]
        y_conv = a * jax.nn.silu(conv_gate)

        q = _rotary(q)
        k = _rotary(k) * (RET_QK_DIM ** -0.5)
        r = _retention_chunkwise(q, k, v)
        r = _head_groupnorm(r, ret_gn_g[l], ret_gn_b[l]).astype(dt)
        y_ret = r * jax.nn.silu(ret_gate)

        y = jnp.concatenate([y_conv, y_ret], axis=-1) @ w_out[l]
        h = h + gate * y
    return _rmsnorm(h, final_g)
```

```python
import numpy as np
import jax
import jax.numpy as jnp
from jax import lax
from jax.experimental import pallas as pl
from jax.experimental.pallas import tpu as pltpu

D_MODEL = 1024
D_CONV = 1024
D_RET = 1024
HEADS = 4
DK = 256
DV = 256
CONV_W = 31
ROPE_BASE = 10000.0
EPS = 1e-6
N_IN = 3 * D_CONV + 2 * HEADS * DK + 2 * D_RET

O_A, O_B, O_G = 0, D_CONV, 2 * D_CONV
O_Q = 3 * D_CONV
O_K = O_Q + HEADS * DK
O_V = O_K + HEADS * DK
O_RG = O_V + D_RET

SUBLANES = 8
LANES = 128
TS = 256
HALO = 32
RB = 32
MOD_TN = 512
VMEM_LIMIT = 56 * 1024 * 1024

TAP_OFF = [k + HALO - (CONV_W - 1) for k in range(CONV_W)]
SH_ROWS = TS + HALO - SUBLANES

F32 = jnp.float32
BF16 = jnp.bfloat16


def _sigmoid(v):
    return 1.0 / (1.0 + jnp.exp(-v))


def _silu(v):
    return v * _sigmoid(v)


def _mod_kernel(c_ref, w_ref, b_ref, o_ref):
    c_act = _silu(c_ref[...])
    o_ref[...] = jnp.dot(c_act, w_ref[...], preferred_element_type=F32,
                         precision=lax.Precision.HIGHEST) + b_ref[...]


def _adaln_mod(c, ada_w, ada_b):
    bsz = c.shape[0]
    n = ada_w.shape[1]
    return pl.pallas_call(
        _mod_kernel,
        out_shape=jax.ShapeDtypeStruct((bsz, n), F32),
        grid=(n // MOD_TN,),
        in_specs=[
            pl.BlockSpec((bsz, D_MODEL), lambda j: (0, 0)),
            pl.BlockSpec((D_MODEL, MOD_TN), lambda j: (0, j)),
            pl.BlockSpec((1, MOD_TN), lambda j: (0, j)),
        ],
        out_specs=pl.BlockSpec((bsz, MOD_TN), lambda j: (0, j)),
        name="adaln_mod",
    )(c, ada_w, ada_b.reshape(1, n))


def _layer_kernel(x_ref, mod_ref, ng_ref, win_ref, cw_ref, cb_ref, lng_ref, lnb_ref,
                  pw_ref, gng_ref, gnb_ref, wout_ref, fg_ref, cos_ref, sin_ref,
                  mask_ref, qd_ref, kd_ref, cd_ref,
                  out_ref,
                  u_buf, proj_buf, a_buf, sh_buf, conv_buf, c_buf, pwo_buf,
                  q_buf, kdec_buf, k_buf, v_buf, ycat_buf, y_buf, state):
    t = pl.program_id(1)
    nrb = TS // RB

    @pl.when(t == 0)
    def _reset():
        state[...] = jnp.zeros_like(state)
        a_buf[0:HALO, :] = jnp.zeros((HALO, D_CONV), F32)

    shift = mod_ref[0, 0:1, :]
    scale1 = 1.0 + mod_ref[0, 1:2, :]
    gate = mod_ref[0, 2:3, :]
    ng = ng_ref[...]

    def norm_rows(i, carry):
        r = pl.multiple_of(i * RB, RB)
        xb = x_ref[0, pl.ds(r, RB), :]
        ms = jnp.mean(xb * xb, axis=-1, keepdims=True)
        y = xb * lax.rsqrt(ms + EPS) * ng
        u_buf[pl.ds(r, RB), :] = (y * scale1 + shift).astype(BF16)
        return carry
    lax.fori_loop(0, nrb, norm_rows, 0)

    proj_buf[...] = jnp.dot(u_buf[...], win_ref[...], preferred_element_type=F32)

    def glu_rows(i, carry):
        r = pl.multiple_of(i * RB, RB)
        pa = proj_buf[pl.ds(r, RB), O_A:O_A + D_CONV]
        pb = proj_buf[pl.ds(r, RB), O_B:O_B + D_CONV]
        a_buf[pl.ds(HALO + r, RB), :] = pa * _sigmoid(pb)
        return carry
    lax.fori_loop(0, nrb, glu_rows, 0)

    for res in range(1, SUBLANES):
        sh_buf[res - 1, :, :] = a_buf[res:res + SH_ROWS, :]

    def conv_rows(i, carry):
        r = pl.multiple_of(i * RB, RB)
        for s in range(D_CONV // LANES):
            ls = slice(s * LANES, (s + 1) * LANES)
            acc = jnp.broadcast_to(cb_ref[:, ls], (RB, LANES))
            for res in range(SUBLANES):
                taps = [(k, off // SUBLANES) for k, off in enumerate(TAP_OFF)
                        if off % SUBLANES == res]
                span = RB + SUBLANES * max(q for _, q in taps)
                if res == 0:
                    win = a_buf[pl.ds(r, span), ls]
                else:
                    win = sh_buf[res - 1, pl.ds(r, span), ls]
                for k, q in taps:
                    acc = acc + win[q * SUBLANES:q * SUBLANES + RB, :] * cw_ref[k:k + 1, ls]
            conv_buf[pl.ds(r, RB), ls] = acc
        return carry
    lax.fori_loop(0, nrb, conv_rows, 0)
    a_buf[0:HALO, :] = a_buf[TS:TS + HALO, :]

    lng = lng_ref[...]
    lnb = lnb_ref[...]

    def ln_rows(i, carry):
        r = pl.multiple_of(i * RB, RB)
        cv = conv_buf[pl.ds(r, RB), :]
        mu = jnp.mean(cv, axis=-1, keepdims=True)
        cen = cv - mu
        var = jnp.mean(cen * cen, axis=-1, keepdims=True)
        ln = cen * lax.rsqrt(var + EPS) * lng + lnb
        c_buf[pl.ds(r, RB), :] = _silu(ln).astype(BF16)
        return carry
    lax.fori_loop(0, nrb, ln_rows, 0)

    pwo_buf[...] = jnp.dot(c_buf[...], pw_ref[...], preferred_element_type=F32)

    def convgate_rows(i, carry):
        r = pl.multiple_of(i * RB, RB)
        g = proj_buf[pl.ds(r, RB), O_G:O_G + D_CONV]
        ycat_buf[pl.ds(r, RB), 0:D_CONV] = (pwo_buf[pl.ds(r, RB), :] * _silu(g)).astype(BF16)
        return carry
    lax.fori_loop(0, nrb, convgate_rows, 0)

    half = DK // 2
    kscale = DK ** -0.5

    def rope_rows(i, carry):
        r = pl.multiple_of(i * RB, RB)
        cs = cos_ref[pl.ds(r, RB), :]
        sn = sin_ref[pl.ds(r, RB), :]
        for h in range(HEADS):
            lo = slice(h * DK, h * DK + half)
            hi = slice(h * DK + half, (h + 1) * DK)
            q1 = proj_buf[pl.ds(r, RB), O_Q + h * DK:O_Q + h * DK + half]
            q2 = proj_buf[pl.ds(r, RB), O_Q + h * DK + half:O_Q + (h + 1) * DK]
            q_buf[pl.ds(r, RB), lo] = (q1 * cs - q2 * sn).astype(BF16)
            q_buf[pl.ds(r, RB), hi] = (q1 * sn + q2 * cs).astype(BF16)
            k1 = proj_buf[pl.ds(r, RB), O_K + h * DK:O_K + h * DK + half]
            k2 = proj_buf[pl.ds(r, RB), O_K + h * DK + half:O_K + (h + 1) * DK]
            o1 = (k1 * cs - k2 * sn) * kscale
            o2 = (k1 * sn + k2 * cs) * kscale
            k_buf[pl.ds(r, RB), lo] = o1.astype(BF16)
            k_buf[pl.ds(r, RB), hi] = o2.astype(BF16)
            kdr = kd_ref[h, pl.ds(r, RB), :]
            kdec_buf[pl.ds(r, RB), lo] = (o1 * kdr).astype(BF16)
            kdec_buf[pl.ds(r, RB), hi] = (o2 * kdr).astype(BF16)
        v_buf[pl.ds(r, RB), :] = proj_buf[pl.ds(r, RB), O_V:O_V + D_RET].astype(BF16)
        return carry
    lax.fori_loop(0, nrb, rope_rows, 0)

    for h in range(HEADS):
        hs = slice(h * DK, (h + 1) * DK)
        q = q_buf[:, hs]
        v = v_buf[:, hs]
        st = state[h]
        scores = lax.dot_general(q, k_buf[:, hs], (((1,), (1,)), ((), ())),
                                 preferred_element_type=F32) * mask_ref[h]
        inner = jnp.dot(scores.astype(BF16), v, preferred_element_type=F32)
        cross = jnp.dot(q, st.astype(BF16), preferred_element_type=F32) * qd_ref[h]
        upd = lax.dot_general(kdec_buf[:, hs], v, (((0,), (0,)), ((), ())),
                              preferred_element_type=F32)
        state[h] = st * cd_ref[h] + upd
        o = inner + cross
        mu = jnp.mean(o, axis=-1, keepdims=True)
        cen = o - mu
        var = jnp.mean(cen * cen, axis=-1, keepdims=True)
        gn = cen * lax.rsqrt(var + EPS) * gng_ref[:, hs] + gnb_ref[:, hs]
        rg = proj_buf[:, O_RG + h * DV:O_RG + (h + 1) * DV]
        ycat_buf[:, D_CONV + h * DV:D_CONV + (h + 1) * DV] = (gn * _silu(rg)).astype(BF16)

    y_buf[...] = jnp.dot(ycat_buf[...], wout_ref[...], preferred_element_type=F32)
    fg = fg_ref[...]

    def out_rows(i, carry):
        r = pl.multiple_of(i * RB, RB)
        hres = x_ref[0, pl.ds(r, RB), :] + gate * y_buf[pl.ds(r, RB), :]
        ms = jnp.mean(hres * hres, axis=-1, keepdims=True)
        out_ref[0, pl.ds(r, RB), :] = hres * lax.rsqrt(ms + EPS) * fg
        return carry
    lax.fori_loop(0, nrb, out_rows, 0)


def _retention_tables():
    log_g = np.log(1.0 - np.exp2(-5.0 - np.arange(HEADS, dtype=np.float64)))
    idx = np.arange(TS, dtype=np.float64)
    diff = idx[:, None] - idx[None, :]
    mask = np.where(diff >= 0, np.exp(log_g[:, None, None] * np.maximum(diff, 0.0)[None]), 0.0)
    qd = np.exp(log_g[:, None] * (idx + 1.0)[None, :])
    kd = np.exp(log_g[:, None] * (TS - 1.0 - idx)[None, :])
    cd = np.exp(log_g * TS)
    qd = np.broadcast_to(qd[:, :, None], (HEADS, TS, DV))
    kd = np.broadcast_to(kd[:, :, None], (HEADS, TS, DK // 2))
    cd = np.broadcast_to(cd[:, None, None], (HEADS, 1, DV))
    return tuple(jnp.asarray(np.ascontiguousarray(a), dtype=F32) for a in (mask, qd, kd, cd))


def _rope_tables(seq):
    inv_freq = 1.0 / (ROPE_BASE ** np.linspace(0.0, 1.0, DK // 2, dtype=np.float64))
    theta = np.arange(seq, dtype=np.float64)[:, None] * inv_freq[None, :]
    return jnp.asarray(np.cos(theta), dtype=F32), jnp.asarray(np.sin(theta), dtype=F32)


def _const_spec(shape):
    nd = len(shape)
    return pl.BlockSpec(shape, lambda b, t: (0,) * nd, pipeline_mode=pl.Buffered(1))


def kernel(x, c, ada_w, ada_b, norm_g, w_in, conv_w, conv_b, conv_ln_g, conv_ln_b,
           conv_pw, ret_gn_g, ret_gn_b, w_out, final_g):
    bsz, seq, _ = x.shape
    assert seq % TS == 0 and TS % RB == 0

    mod = _adaln_mod(c, ada_w[0], ada_b[0]).reshape(bsz, 3, D_MODEL)
    cos, sin = _rope_tables(seq)
    mask, qd, kd, cd = _retention_tables()
    row = lambda a: a.reshape(1, -1).astype(F32)

    operands = [
        (x, pl.BlockSpec((1, TS, D_MODEL), lambda b, t: (b, t, 0))),
        (mod, pl.BlockSpec((1, 3, D_MODEL), lambda b, t: (b, 0, 0))),
        (row(norm_g[0]), _const_spec((1, D_MODEL))),
        (w_in[0].astype(BF16), _const_spec((D_MODEL, N_IN))),
        (conv_w[0], _const_spec((CONV_W, D_CONV))),
        (row(conv_b[0]), _const_spec((1, D_CONV))),
        (row(conv_ln_g[0]), _const_spec((1, D_CONV))),
        (row(conv_ln_b[0]), _const_spec((1, D_CONV))),
        (conv_pw[0].astype(BF16), _const_spec((D_CONV, D_CONV))),
        (row(ret_gn_g[0]), _const_spec((1, D_RET))),
        (row(ret_gn_b[0]), _const_spec((1, D_RET))),
        (w_out[0].astype(BF16), _const_spec((D_CONV + D_RET, D_MODEL))),
        (row(final_g), _const_spec((1, D_MODEL))),
        (cos, pl.BlockSpec((TS, DK // 2), lambda b, t: (t, 0))),
        (sin, pl.BlockSpec((TS, DK // 2), lambda b, t: (t, 0))),
        (mask, _const_spec((HEADS, TS, TS))),
        (qd, _const_spec((HEADS, TS, DV))),
        (kd, _const_spec((HEADS, TS, DK // 2))),
        (cd, _const_spec((HEADS, 1, DV))),
    ]
    arrays = [a for a, _ in operands]
    in_specs = [s for _, s in operands]

    scratch = [
        pltpu.VMEM((TS, D_MODEL), BF16),
        pltpu.VMEM((TS, N_IN), F32),
        pltpu.VMEM((TS + HALO, D_CONV), F32),
        pltpu.VMEM((SUBLANES - 1, SH_ROWS, D_CONV), F32),
        pltpu.VMEM((TS, D_CONV), F32),
        pltpu.VMEM((TS, D_CONV), BF16),
        pltpu.VMEM((TS, D_CONV), F32),
        pltpu.VMEM((TS, HEADS * DK), BF16),
        pltpu.VMEM((TS, HEADS * DK), BF16),
        pltpu.VMEM((TS, HEADS * DK), BF16),
        pltpu.VMEM((TS, D_RET), BF16),
        pltpu.VMEM((TS, D_CONV + D_RET), BF16),
        pltpu.VMEM((TS, D_MODEL), F32),
        pltpu.VMEM((HEADS, DK, DV), F32),
    ]

    return pl.pallas_call(
        _layer_kernel,
        out_shape=jax.ShapeDtypeStruct((bsz, seq, D_MODEL), x.dtype),
        grid=(bsz, seq // TS),
        in_specs=in_specs,
        out_specs=pl.BlockSpec((1, TS, D_MODEL), lambda b, t: (b, t, 0)),
        scratch_shapes=scratch,
        compiler_params=pltpu.CompilerParams(
            dimension_semantics=("arbitrary", "arbitrary"),
            vmem_limit_bytes=VMEM_LIMIT),
        name="hybrid_layer",
    )(*arrays)
```

```python
import numpy as np
import jax
import jax.numpy as jnp
from jax import lax
from jax.experimental import pallas as pl
from jax.experimental.pallas import tpu as pltpu

D_MODEL = 1024
D_CONV = 1024
D_RET = 1024
HEADS = 4
DK = 256
DV = 256
CONV_W = 31
ROPE_BASE = 10000.0
EPS = 1e-6
N_IN = 3 * D_CONV + 2 * HEADS * DK + 2 * D_RET

O_A, O_B, O_G = 0, D_CONV, 2 * D_CONV
O_Q = 3 * D_CONV
O_K = O_Q + HEADS * DK
O_V = O_K + HEADS * DK
O_RG = O_V + D_RET

SUBLANES = 8
LANES = 128
TS = 256
HALO = 32
RB = 16
CRB = 32
MOD_TN = 512
VMEM_LIMIT = 56 * 1024 * 1024

TAP_OFF = [k + HALO - (CONV_W - 1) for k in range(CONV_W)]
SH_ROWS = TS + HALO - SUBLANES

F32 = jnp.float32
BF16 = jnp.bfloat16


def _sigmoid(v):
    return 1.0 / (1.0 + jnp.exp(-v))


def _silu(v):
    return v * _sigmoid(v)


def _mod_kernel(c_ref, w_ref, b_ref, o_ref):
    c_act = _silu(c_ref[...])
    o_ref[...] = jnp.dot(c_act, w_ref[...], preferred_element_type=F32,
                         precision=lax.Precision.HIGHEST) + b_ref[...]


def _adaln_mod(c, ada_w, ada_b):
    bsz = c.shape[0]
    n = ada_w.shape[1]
    return pl.pallas_call(
        _mod_kernel,
        out_shape=jax.ShapeDtypeStruct((bsz, n), F32),
        grid=(n // MOD_TN,),
        in_specs=[
            pl.BlockSpec((bsz, D_MODEL), lambda j: (0, 0)),
            pl.BlockSpec((D_MODEL, MOD_TN), lambda j: (0, j)),
            pl.BlockSpec((1, MOD_TN), lambda j: (0, j)),
        ],
        out_specs=pl.BlockSpec((bsz, MOD_TN), lambda j: (0, j)),
        name="adaln_mod",
    )(c, ada_w, ada_b.reshape(1, n))


def _layer_kernel(x_ref, mod_ref, ng_ref, win_ref, cw_ref, cb_ref, lng_ref, lnb_ref,
                  pw_ref, gng_ref, gnb_ref, wout_ref, fg_ref, cos_ref, sin_ref,
                  mask_ref, qd_ref, kd_ref, cd_ref,
                  out_ref,
                  u_buf, proj_buf, a_buf, sh_buf, conv_buf, c_buf, pwo_buf,
                  q_buf, kdec_buf, k_buf, v_buf, ycat_buf, y_buf, state):
    t = pl.program_id(1)
    row_blocks = [slice(i * RB, (i + 1) * RB) for i in range(TS // RB)]

    @pl.when(t == 0)
    def _reset():
        state[...] = jnp.zeros_like(state)
        a_buf[0:HALO, :] = jnp.zeros((HALO, D_CONV), F32)

    shift = mod_ref[0, 0:1, :]
    scale1 = 1.0 + mod_ref[0, 1:2, :]
    gate = mod_ref[0, 2:3, :]
    ng = ng_ref[...]

    for rs in row_blocks:
        xb = x_ref[0, rs, :]
        ms = jnp.mean(xb * xb, axis=-1, keepdims=True)
        y = xb * lax.rsqrt(ms + EPS) * ng
        u_buf[rs, :] = (y * scale1 + shift).astype(BF16)

    n_ab = 2 * D_CONV
    proj_buf[:, 0:n_ab] = jnp.dot(u_buf[...], win_ref[:, 0:n_ab], preferred_element_type=F32)

    for rs in row_blocks:
        pa = proj_buf[rs, O_A:O_A + D_CONV]
        pb = proj_buf[rs, O_B:O_B + D_CONV]
        a_buf[HALO + rs.start:HALO + rs.stop, :] = pa * _sigmoid(pb)

    proj_buf[:, n_ab:N_IN] = jnp.dot(u_buf[...], win_ref[:, n_ab:N_IN],
                                     preferred_element_type=F32)

    for s in range(D_CONV // LANES):
        ls = slice(s * LANES, (s + 1) * LANES)
        slot = s % 2
        for res in range(1, SUBLANES):
            sh_buf[slot, res - 1, :, :] = a_buf[res:res + SH_ROWS, ls]
        for r in range(0, TS, CRB):
            acc = jnp.broadcast_to(cb_ref[:, ls], (CRB, LANES))
            for res in range(SUBLANES):
                taps = [(k, off // SUBLANES) for k, off in enumerate(TAP_OFF)
                        if off % SUBLANES == res]
                span = CRB + SUBLANES * max(q for _, q in taps)
                if res == 0:
                    win = a_buf[r:r + span, ls]
                else:
                    win = sh_buf[slot, res - 1, r:r + span, :]
                for k, q in taps:
                    acc = acc + win[q * SUBLANES:q * SUBLANES + CRB, :] * cw_ref[k:k + 1, ls]
            conv_buf[r:r + CRB, ls] = acc
    a_buf[0:HALO, :] = a_buf[TS:TS + HALO, :]

    lng = lng_ref[...]
    lnb = lnb_ref[...]
    for rs in row_blocks:
        cv = conv_buf[rs, :]
        mu = jnp.mean(cv, axis=-1, keepdims=True)
        cen = cv - mu
        var = jnp.mean(cen * cen, axis=-1, keepdims=True)
        ln = cen * lax.rsqrt(var + EPS) * lng + lnb
        c_buf[rs, :] = _silu(ln).astype(BF16)

    pwo_buf[...] = jnp.dot(c_buf[...], pw_ref[...], preferred_element_type=F32)
    for rs in row_blocks:
        g = proj_buf[rs, O_G:O_G + D_CONV]
        ycat_buf[rs, 0:D_CONV] = (pwo_buf[rs, :] * _silu(g)).astype(BF16)

    half = DK // 2
    kscale = DK ** -0.5
    for rs in row_blocks:
        cs = cos_ref[rs, :]
        sn = sin_ref[rs, :]
        for h in range(HEADS):
            lo = slice(h * DK, h * DK + half)
            hi = slice(h * DK + half, (h + 1) * DK)
            q1 = proj_buf[rs, O_Q + h * DK:O_Q + h * DK + half]
            q2 = proj_buf[rs, O_Q + h * DK + half:O_Q + (h + 1) * DK]
            q_buf[rs, lo] = (q1 * cs - q2 * sn).astype(BF16)
            q_buf[rs, hi] = (q1 * sn + q2 * cs).astype(BF16)
            k1 = proj_buf[rs, O_K + h * DK:O_K + h * DK + half]
            k2 = proj_buf[rs, O_K + h * DK + half:O_K + (h + 1) * DK]
            o1 = (k1 * cs - k2 * sn) * kscale
            o2 = (k1 * sn + k2 * cs) * kscale
            k_buf[rs, lo] = o1.astype(BF16)
            k_buf[rs, hi] = o2.astype(BF16)
            kdr = kd_ref[h, rs, :]
            kdec_buf[rs, lo] = (o1 * kdr).astype(BF16)
            kdec_buf[rs, hi] = (o2 * kdr).astype(BF16)
        v_buf[rs, :] = proj_buf[rs, O_V:O_V + D_RET].astype(BF16)

    for h in range(HEADS):
        hs = slice(h * DK, (h + 1) * DK)
        q = q_buf[:, hs]
        v = v_buf[:, hs]
        st = state[h]
        scores = lax.dot_general(q, k_buf[:, hs], (((1,), (1,)), ((), ())),
                                 preferred_element_type=F32) * mask_ref[h]
        inner = jnp.dot(scores.astype(BF16), v, preferred_element_type=F32)
        qd = qd_ref[h]
        cross = jnp.dot(q, st.astype(BF16), preferred_element_type=F32)
        upd = lax.dot_general(kdec_buf[:, hs], v, (((0,), (0,)), ((), ())),
                              preferred_element_type=F32)
        state[h] = st * cd_ref[h] + upd
        o = inner + cross * jnp.concatenate([qd, qd], axis=-1)
        mu = jnp.mean(o, axis=-1, keepdims=True)
        cen = o - mu
        var = jnp.mean(cen * cen, axis=-1, keepdims=True)
        gn = cen * lax.rsqrt(var + EPS) * gng_ref[:, hs] + gnb_ref[:, hs]
        rg = proj_buf[:, O_RG + h * DV:O_RG + (h + 1) * DV]
        ycat_buf[:, D_CONV + h * DV:D_CONV + (h + 1) * DV] = (gn * _silu(rg)).astype(BF16)

    y_buf[...] = jnp.dot(ycat_buf[...], wout_ref[...], preferred_element_type=F32)
    fg = fg_ref[...]
    for rs in row_blocks:
        hres = x_ref[0, rs, :] + gate * y_buf[rs, :]
        ms = jnp.mean(hres * hres, axis=-1, keepdims=True)
        out_ref[0, rs, :] = hres * lax.rsqrt(ms + EPS) * fg


def _retention_tables():
    log_g = np.log(1.0 - np.exp2(-5.0 - np.arange(HEADS, dtype=np.float64)))
    idx = np.arange(TS, dtype=np.float64)
    diff = idx[:, None] - idx[None, :]
    mask = np.where(diff >= 0, np.exp(log_g[:, None, None] * np.maximum(diff, 0.0)[None]), 0.0)
    qd = np.exp(log_g[:, None] * (idx + 1.0)[None, :])
    kd = np.exp(log_g[:, None] * (TS - 1.0 - idx)[None, :])
    cd = np.exp(log_g * TS)
    qd = np.broadcast_to(qd[:, :, None], (HEADS, TS, DV // 2))
    kd = np.broadcast_to(kd[:, :, None], (HEADS, TS, DK // 2))
    cd = np.broadcast_to(cd[:, None, None], (HEADS, 1, DV))
    return tuple(jnp.asarray(np.ascontiguousarray(a), dtype=F32) for a in (mask, qd, kd, cd))


def _rope_tables(seq):
    inv_freq = 1.0 / (ROPE_BASE ** np.linspace(0.0, 1.0, DK // 2, dtype=np.float64))
    theta = np.arange(seq, dtype=np.float64)[:, None] * inv_freq[None, :]
    return jnp.asarray(np.cos(theta), dtype=F32), jnp.asarray(np.sin(theta), dtype=F32)


def _const_spec(shape):
    nd = len(shape)
    return pl.BlockSpec(shape, lambda b, t: (0,) * nd, pipeline_mode=pl.Buffered(1))


def kernel(x, c, ada_w, ada_b, norm_g, w_in, conv_w, conv_b, conv_ln_g, conv_ln_b,
           conv_pw, ret_gn_g, ret_gn_b, w_out, final_g):
    bsz, seq, _ = x.shape
    assert seq % TS == 0 and TS % RB == 0 and TS % CRB == 0

    mod = _adaln_mod(c, ada_w[0], ada_b[0]).reshape(bsz, 3, D_MODEL)
    cos, sin = _rope_tables(seq)
    mask, qd, kd, cd = _retention_tables()
    row = lambda a: a.reshape(1, -1).astype(F32)

    operands = [
        (x, pl.BlockSpec((1, TS, D_MODEL), lambda b, t: (b, t, 0))),
        (mod, pl.BlockSpec((1, 3, D_MODEL), lambda b, t: (b, 0, 0))),
        (row(norm_g[0]), _const_spec((1, D_MODEL))),
        (w_in[0].astype(BF16), _const_spec((D_MODEL, N_IN))),
        (conv_w[0], _const_spec((CONV_W, D_CONV))),
        (row(conv_b[0]), _const_spec((1, D_CONV))),
        (row(conv_ln_g[0]), _const_spec((1, D_CONV))),
        (row(conv_ln_b[0]), _const_spec((1, D_CONV))),
        (conv_pw[0].astype(BF16), _const_spec((D_CONV, D_CONV))),
        (row(ret_gn_g[0]), _const_spec((1, D_RET))),
        (row(ret_gn_b[0]), _const_spec((1, D_RET))),
        (w_out[0].astype(BF16), _const_spec((D_CONV + D_RET, D_MODEL))),
        (row(final_g), _const_spec((1, D_MODEL))),
        (cos, pl.BlockSpec((TS, DK // 2), lambda b, t: (t, 0))),
        (sin, pl.BlockSpec((TS, DK // 2), lambda b, t: (t, 0))),
        (mask, _const_spec((HEADS, TS, TS))),
        (qd, _const_spec((HEADS, TS, DV // 2))),
        (kd, _const_spec((HEADS, TS, DK // 2))),
        (cd, _const_spec((HEADS, 1, DV))),
    ]
    arrays = [a for a, _ in operands]
    in_specs = [s for _, s in operands]

    scratch = [
        pltpu.VMEM((TS, D_MODEL), BF16),
        pltpu.VMEM((TS, N_IN), F32),
        pltpu.VMEM((TS + HALO, D_CONV), F32),
        pltpu.VMEM((2, SUBLANES - 1, SH_ROWS, LANES), F32),
        pltpu.VMEM((TS, D_CONV), F32),
        pltpu.VMEM((TS, D_CONV), BF16),
        pltpu.VMEM((TS, D_CONV), F32),
        pltpu.VMEM((TS, HEADS * DK), BF16),
        pltpu.VMEM((TS, HEADS * DK), BF16),
        pltpu.VMEM((TS, HEADS * DK), BF16),
        pltpu.VMEM((TS, D_RET), BF16),
        pltpu.VMEM((TS, D_CONV + D_RET), BF16),
        pltpu.VMEM((TS, D_MODEL), F32),
        pltpu.VMEM((HEADS, DK, DV), F32),
    ]

    return pl.pallas_call(
        _layer_kernel,
        out_shape=jax.ShapeDtypeStruct((bsz, seq, D_MODEL), x.dtype),
        grid=(bsz, seq // TS),
        in_specs=in_specs,
        out_specs=pl.BlockSpec((1, TS, D_MODEL), lambda b, t: (b, t, 0)),
        scratch_shapes=scratch,
        compiler_params=pltpu.CompilerParams(
            dimension_semantics=("arbitrary", "arbitrary"),
            vmem_limit_bytes=VMEM_LIMIT),
        name="hybrid_layer",
    )(*arrays)
```

```python
import numpy as np
import jax
import jax.numpy as jnp
from jax import lax
from jax.experimental import pallas as pl
from jax.experimental.pallas import tpu as pltpu

D_MODEL = 1024
D_CONV = 1024
D_RET = 1024
HEADS = 4
DK = 256
DV = 256
CONV_W = 31
ROPE_BASE = 10000.0
EPS = 1e-6
N_IN = 3 * D_CONV + 2 * HEADS * DK + 2 * D_RET

SUBLANES = 8
LANES = 128
MXU_N = 256
TS = 256
HALO = 32
RB = 16
CRB = 32
MOD_TN = 512
VMEM_LIMIT = 56 * 1024 * 1024

NSLICE = D_CONV // MXU_N
AB_W = 2 * MXU_N
GROUP_W = D_CONV
G_Q, G_K, G_V, G_GATE, G_RG = range(5)
NGROUP = 5
NGROUP_LOOP = 4
NSTRIP = D_CONV // LANES
STRIPS_PER_SLICE = NSTRIP // NSLICE
STRIPS_PER_GROUP = NSTRIP // NGROUP_LOOP
assert GROUP_W == HEADS * DK == D_RET

TAP_OFF = [k + HALO - (CONV_W - 1) for k in range(CONV_W)]
SH_ROWS = TS + HALO - SUBLANES

F32 = jnp.float32
BF16 = jnp.bfloat16


def _sigmoid(v):
    return 1.0 / (1.0 + jnp.exp(-v))


def _silu(v):
    return v * _sigmoid(v)


def _mod_kernel(c_ref, w_ref, b_ref, o_ref):
    c_act = _silu(c_ref[...])
    o_ref[...] = jnp.dot(c_act, w_ref[...], preferred_element_type=F32,
                         precision=lax.Precision.HIGHEST) + b_ref[...]


def _adaln_mod(c, ada_w, ada_b):
    bsz = c.shape[0]
    n = ada_w.shape[1]
    return pl.pallas_call(
        _mod_kernel,
        out_shape=jax.ShapeDtypeStruct((bsz, n), F32),
        grid=(n // MOD_TN,),
        in_specs=[
            pl.BlockSpec((bsz, D_MODEL), lambda j: (0, 0)),
            pl.BlockSpec((D_MODEL, MOD_TN), lambda j: (0, j)),
            pl.BlockSpec((1, MOD_TN), lambda j: (0, j)),
        ],
        out_specs=pl.BlockSpec((bsz, MOD_TN), lambda j: (0, j)),
        name="adaln_mod",
    )(c, ada_w, ada_b.reshape(1, n))


def _layer_kernel(x_ref, mod_ref, ng_ref, wab_ref, wrest_ref, cw_ref, cb_ref, lng_ref, lnb_ref,
                  pw_ref, gng_ref, gnb_ref, wout_ref, fg_ref, cos_ref, sin_ref,
                  mask_ref, qd_ref, kd_ref, cd_ref,
                  out_ref,
                  u_buf, a_buf, sh_buf, prest_buf, conv_buf, c_buf,
                  q_buf, kdec_buf, k_buf, v_buf, ycat_buf, state):
    t = pl.program_id(1)
    row_blocks = [slice(i * RB, (i + 1) * RB) for i in range(TS // RB)]

    @pl.when(t == 0)
    def _reset():
        state[...] = jnp.zeros_like(state)
        a_buf[:, 0:HALO, :] = jnp.zeros((NSTRIP, HALO, LANES), F32)

    shift = mod_ref[0, 0:1, :]
    scale1 = 1.0 + mod_ref[0, 1:2, :]
    gate = mod_ref[0, 2:3, :]
    ng = ng_ref[...]

    for rs in row_blocks:
        xb = x_ref[0, rs, :]
        ms = jnp.mean(xb * xb, axis=-1, keepdims=True)
        y = xb * lax.rsqrt(ms + EPS) * ng
        u_buf[rs, :] = (y * scale1 + shift).astype(BF16)

    def glu_and_shift(j, pab):
        for e in range(STRIPS_PER_SLICE):
            s = j * STRIPS_PER_SLICE + e
            pa = pab[:, e * LANES:(e + 1) * LANES]
            pb = pab[:, MXU_N + e * LANES:MXU_N + (e + 1) * LANES]
            a_buf[s, HALO:HALO + TS, :] = pa * _sigmoid(pb)
            for res in range(1, SUBLANES):
                sh_buf[s, res - 1, :, :] = a_buf[s, res:res + SH_ROWS, :]

    u_all = u_buf[...]
    pab_prev = jnp.dot(u_all, wab_ref[0], preferred_element_type=F32)
    for j in range(1, NSLICE):
        pab_next = jnp.dot(u_all, wab_ref[j], preferred_element_type=F32)
        glu_and_shift(j - 1, pab_prev)
        pab_prev = pab_next
    glu_and_shift(NSLICE - 1, pab_prev)

    def zero_from(v):
        bits = pltpu.bitcast(v, jnp.uint32)
        sh16 = jnp.uint32(16)
        return pltpu.bitcast(lax.shift_right_logical(lax.shift_right_logical(bits, sh16), sh16), F32)

    dyn_zero = jnp.minimum(t, 0)
    nrep = CRB // SUBLANES

    def conv_strip(s_static, token):
        s = s_static + dyn_zero
        for r in range(0, TS, CRB):
            z = zero_from(token)
            cb8 = jnp.broadcast_to(cb_ref[s], (SUBLANES, LANES)) + z
            acc = jnp.concatenate([cb8] * nrep, axis=0)
            for res in range(SUBLANES):
                taps = [(k, off // SUBLANES) for k, off in enumerate(TAP_OFF)
                        if off % SUBLANES == res]
                span = CRB + SUBLANES * max(q for _, q in taps)
                if res == 0:
                    win = a_buf[s, r:r + span, :]
                else:
                    win = sh_buf[s, res - 1, r:r + span, :]
                for k, q in taps:
                    wk = jnp.broadcast_to(cw_ref[s, k:k + 1, :], (SUBLANES, LANES)) + z
                    acc = acc + (win[q * SUBLANES:q * SUBLANES + CRB, :]
                                 * jnp.concatenate([wk] * nrep, axis=0))
            conv_buf[s, r:r + CRB, :] = acc
            token = acc[0:SUBLANES, :]
        return token

    token = x_ref[0, 0:SUBLANES, 0:LANES]
    for g in range(NGROUP_LOOP):
        prest_buf[g] = jnp.dot(u_all, wrest_ref[g], preferred_element_type=F32)
        for e in range(STRIPS_PER_GROUP):
            token = conv_strip(g * STRIPS_PER_GROUP + e, token)
    prest_buf[G_RG] = jnp.dot(u_all, wrest_ref[G_RG], preferred_element_type=F32)
    a_buf[:, 0:HALO, :] = a_buf[:, TS:TS + HALO, :]

    lng = lng_ref[...]
    lnb = lnb_ref[...]
    for rs in row_blocks:
        cv = jnp.concatenate([conv_buf[s, rs, :] for s in range(NSTRIP)], axis=-1)
        mu = jnp.mean(cv, axis=-1, keepdims=True)
        cen = cv - mu
        var = jnp.mean(cen * cen, axis=-1, keepdims=True)
        ln = cen * lax.rsqrt(var + EPS) * lng + lnb
        c_buf[rs, :] = _silu(ln).astype(BF16)

    pwo = jnp.dot(c_buf[...], pw_ref[...], preferred_element_type=F32)
    for rs in row_blocks:
        ycat_buf[rs, 0:D_CONV] = (pwo[rs, :] * _silu(prest_buf[G_GATE, rs, :])).astype(BF16)

    half = DK // 2
    kscale = DK ** -0.5
    for rs in row_blocks:
        cs = cos_ref[rs, :]
        sn = sin_ref[rs, :]
        for h in range(HEADS):
            lo = slice(h * DK, h * DK + half)
            hi = slice(h * DK + half, (h + 1) * DK)
            q1 = prest_buf[G_Q, rs, lo]
            q2 = prest_buf[G_Q, rs, hi]
            q_buf[rs, lo] = (q1 * cs - q2 * sn).astype(BF16)
            q_buf[rs, hi] = (q1 * sn + q2 * cs).astype(BF16)
            k1 = prest_buf[G_K, rs, lo]
            k2 = prest_buf[G_K, rs, hi]
            o1 = (k1 * cs - k2 * sn) * kscale
            o2 = (k1 * sn + k2 * cs) * kscale
            k_buf[rs, lo] = o1.astype(BF16)
            k_buf[rs, hi] = o2.astype(BF16)
            kdr = kd_ref[h, rs, :]
            kdec_buf[rs, lo] = (o1 * kdr).astype(BF16)
            kdec_buf[rs, hi] = (o2 * kdr).astype(BF16)
        v_buf[rs, :] = prest_buf[G_V, rs, :].astype(BF16)

    for h in range(HEADS):
        hs = slice(h * DK, (h + 1) * DK)
        q = q_buf[:, hs]
        v = v_buf[:, hs]
        st = state[h]
        scores = lax.dot_general(q, k_buf[:, hs], (((1,), (1,)), ((), ())),
                                 preferred_element_type=F32) * mask_ref[h]
        inner = jnp.dot(scores.astype(BF16), v, preferred_element_type=F32)
        qd = qd_ref[h]
        cross = jnp.dot(q, st.astype(BF16), preferred_element_type=F32)
        upd = lax.dot_general(kdec_buf[:, hs], v, (((0,), (0,)), ((), ())),
                              preferred_element_type=F32)
        state[h] = st * cd_ref[h] + upd
        o = inner + cross * jnp.concatenate([qd, qd], axis=-1)
        mu = jnp.mean(o, axis=-1, keepdims=True)
        cen = o - mu
        var = jnp.mean(cen * cen, axis=-1, keepdims=True)
        gn = cen * lax.rsqrt(var + EPS) * gng_ref[:, hs] + gnb_ref[:, hs]
        rg = prest_buf[G_RG, :, hs]
        ycat_buf[:, D_CONV + h * DV:D_CONV + (h + 1) * DV] = (gn * _silu(rg)).astype(BF16)

    yo = jnp.dot(ycat_buf[...], wout_ref[...], preferred_element_type=F32)
    fg = fg_ref[...]
    for rs in row_blocks:
        hres = x_ref[0, rs, :] + gate * yo[rs, :]
        ms = jnp.mean(hres * hres, axis=-1, keepdims=True)
        out_ref[0, rs, :] = hres * lax.rsqrt(ms + EPS) * fg


def _retention_tables():
    log_g = np.log(1.0 - np.exp2(-5.0 - np.arange(HEADS, dtype=np.float64)))
    idx = np.arange(TS, dtype=np.float64)
    diff = idx[:, None] - idx[None, :]
    mask = np.where(diff >= 0, np.exp(log_g[:, None, None] * np.maximum(diff, 0.0)[None]), 0.0)
    qd = np.exp(log_g[:, None] * (idx + 1.0)[None, :])
    kd = np.exp(log_g[:, None] * (TS - 1.0 - idx)[None, :])
    cd = np.exp(log_g * TS)
    qd = np.broadcast_to(qd[:, :, None], (HEADS, TS, DV // 2))
    kd = np.broadcast_to(kd[:, :, None], (HEADS, TS, DK // 2))
    cd = np.broadcast_to(cd[:, None, None], (HEADS, 1, DV))
    return tuple(jnp.asarray(np.ascontiguousarray(a), dtype=F32) for a in (mask, qd, kd, cd))


def _rope_tables(seq):
    inv_freq = 1.0 / (ROPE_BASE ** np.linspace(0.0, 1.0, DK // 2, dtype=np.float64))
    theta = np.arange(seq, dtype=np.float64)[:, None] * inv_freq[None, :]
    return jnp.asarray(np.cos(theta), dtype=F32), jnp.asarray(np.sin(theta), dtype=F32)


def _split_w_in(w_in):
    w = w_in.astype(BF16)
    n_ab = 2 * D_CONV
    wab = w[:, :n_ab].reshape(D_MODEL, 2, NSLICE, MXU_N).transpose(2, 0, 1, 3)
    groups = w[:, n_ab:].reshape(D_MODEL, NGROUP, GROUP_W)
    order = [None] * NGROUP
    for src, dst in enumerate((G_GATE, G_Q, G_K, G_V, G_RG)):
        order[dst] = src
    wrest = jnp.stack([groups[:, i, :] for i in order], axis=0)
    return wab.reshape(NSLICE, D_MODEL, AB_W), wrest


def _const_spec(shape):
    nd = len(shape)
    return pl.BlockSpec(shape, lambda b, t: (0,) * nd, pipeline_mode=pl.Buffered(1))


def kernel(x, c, ada_w, ada_b, norm_g, w_in, conv_w, conv_b, conv_ln_g, conv_ln_b,
           conv_pw, ret_gn_g, ret_gn_b, w_out, final_g):
    bsz, seq, _ = x.shape
    assert seq % TS == 0 and TS % RB == 0 and TS % CRB == 0

    mod = _adaln_mod(c, ada_w[0], ada_b[0]).reshape(bsz, 3, D_MODEL)
    cos, sin = _rope_tables(seq)
    mask, qd, kd, cd = _retention_tables()
    wab, wrest = _split_w_in(w_in[0])
    cw = conv_w[0].reshape(CONV_W, NSTRIP, LANES).transpose(1, 0, 2)
    cb = conv_b[0].reshape(NSTRIP, 1, LANES)
    row = lambda a: a.reshape(1, -1).astype(F32)

    operands = [
        (x, pl.BlockSpec((1, TS, D_MODEL), lambda b, t: (b, t, 0))),
        (mod, pl.BlockSpec((1, 3, D_MODEL), lambda b, t: (b, 0, 0))),
        (row(norm_g[0]), _const_spec((1, D_MODEL))),
        (wab, _const_spec((NSLICE, D_MODEL, AB_W))),
        (wrest, _const_spec((NGROUP, D_MODEL, GROUP_W))),
        (cw, _const_spec((NSTRIP, CONV_W, LANES))),
        (cb, _const_spec((NSTRIP, 1, LANES))),
        (row(conv_ln_g[0]), _const_spec((1, D_CONV))),
        (row(conv_ln_b[0]), _const_spec((1, D_CONV))),
        (conv_pw[0].astype(BF16), _const_spec((D_CONV, D_CONV))),
        (row(ret_gn_g[0]), _const_spec((1, D_RET))),
        (row(ret_gn_b[0]), _const_spec((1, D_RET))),
        (w_out[0].astype(BF16), _const_spec((D_CONV + D_RET, D_MODEL))),
        (row(final_g), _const_spec((1, D_MODEL))),
        (cos, pl.BlockSpec((TS, DK // 2), lambda b, t: (t, 0))),
        (sin, pl.BlockSpec((TS, DK // 2), lambda b, t: (t, 0))),
        (mask, _const_spec((HEADS, TS, TS))),
        (qd, _const_spec((HEADS, TS, DV // 2))),
        (kd, _const_spec((HEADS, TS, DK // 2))),
        (cd, _const_spec((HEADS, 1, DV))),
    ]
    arrays = [a for a, _ in operands]
    in_specs = [s for _, s in operands]

    scratch = [
        pltpu.VMEM((TS, D_MODEL), BF16),
        pltpu.VMEM((NSTRIP, TS + HALO, LANES), F32),
        pltpu.VMEM((NSTRIP, SUBLANES - 1, SH_ROWS, LANES), F32),
        pltpu.VMEM((NGROUP, TS, GROUP_W), F32),
        pltpu.VMEM((NSTRIP, TS, LANES), F32),
        pltpu.VMEM((TS, D_CONV), BF16),
        pltpu.VMEM((TS, HEADS * DK), BF16),
        pltpu.VMEM((TS, HEADS * DK), BF16),
        pltpu.VMEM((TS, HEADS * DK), BF16),
        pltpu.VMEM((TS, D_RET), BF16),
        pltpu.VMEM((TS, D_CONV + D_RET), BF16),
        pltpu.VMEM((HEADS, DK, DV), F32),
    ]

    return pl.pallas_call(
        _layer_kernel,
        out_shape=jax.ShapeDtypeStruct((bsz, seq, D_MODEL), x.dtype),
        grid=(bsz, seq // TS),
        in_specs=in_specs,
        out_specs=pl.BlockSpec((1, TS, D_MODEL), lambda b, t: (b, t, 0)),
        scratch_shapes=scratch,
        compiler_params=pltpu.CompilerParams(
            dimension_semantics=("arbitrary", "arbitrary"),
            vmem_limit_bytes=VMEM_LIMIT),
        name="hybrid_layer",
    )(*arrays)
```

```python
import numpy as np
import jax
import jax.numpy as jnp
from jax import lax
from jax.experimental import pallas as pl
from jax.experimental.pallas import tpu as pltpu

D_MODEL = 1024
D_CONV = 1024
D_RET = 1024
HEADS = 4
DK = 256
DV = 256
CONV_W = 31
ROPE_BASE = 10000.0
EPS = 1e-6
N_IN = 3 * D_CONV + 2 * HEADS * DK + 2 * D_RET

SUBLANES = 8
LANES = 128
MXU_N = 256
TS = 256
HALO = 32
RB = 16
CRB = 32
MOD_TN = 512
VMEM_LIMIT = 56 * 1024 * 1024

NSLICE = D_CONV // MXU_N
GROUP_W = D_CONV
G_Q, G_K, G_V, G_GATE, G_RG = range(5)
NGROUP = 5
COL_A, COL_B = 0, D_CONV
GROUP_COL = {G_GATE: 2 * D_CONV, G_Q: 3 * D_CONV, G_K: 4 * D_CONV, G_V: 5 * D_CONV,
             G_RG: 6 * D_CONV}
NGROUP_LOOP = 4
NSTRIP = D_CONV // LANES
STRIPS_PER_SLICE = NSTRIP // NSLICE
STRIPS_PER_GROUP = NSTRIP // NGROUP_LOOP
assert GROUP_W == HEADS * DK == D_RET

TAP_OFF = [k + HALO - (CONV_W - 1) for k in range(CONV_W)]
SH_ROWS = TS + HALO - SUBLANES

F32 = jnp.float32
BF16 = jnp.bfloat16


def _sigmoid(v):
    return 1.0 / (1.0 + jnp.exp(-v))


def _silu(v):
    return v * _sigmoid(v)


def _mod_kernel(c_ref, w_ref, b_ref, o_ref):
    c_act = _silu(c_ref[...])
    o_ref[...] = jnp.dot(c_act, w_ref[...], preferred_element_type=F32,
                         precision=lax.Precision.HIGHEST) + b_ref[...]


def _adaln_mod(c, ada_w, ada_b):
    bsz = c.shape[0]
    n = ada_w.shape[1]
    return pl.pallas_call(
        _mod_kernel,
        out_shape=jax.ShapeDtypeStruct((bsz, n), F32),
        grid=(n // MOD_TN,),
        in_specs=[
            pl.BlockSpec((bsz, D_MODEL), lambda j: (0, 0)),
            pl.BlockSpec((D_MODEL, MOD_TN), lambda j: (0, j)),
            pl.BlockSpec((1, MOD_TN), lambda j: (0, j)),
        ],
        out_specs=pl.BlockSpec((bsz, MOD_TN), lambda j: (0, j)),
        name="adaln_mod",
    )(c, ada_w, ada_b.reshape(1, n))


def _layer_kernel(x_ref, mod_ref, ng_ref, win_ref, cw_ref, cb_ref, lng_ref, lnb_ref,
                  pw_ref, gng_ref, gnb_ref, wout_ref, fg_ref, cos_ref, sin_ref,
                  mask_ref, qd_ref, kd_ref, cd_ref,
                  out_ref,
                  u_buf, a_buf, sh_buf, prest_buf, conv_buf, c_buf,
                  q_buf, kdec_buf, k_buf, v_buf, ycat_buf, state):
    t = pl.program_id(1)
    row_blocks = [slice(i * RB, (i + 1) * RB) for i in range(TS // RB)]

    @pl.when(t == 0)
    def _reset():
        state[...] = jnp.zeros_like(state)
        a_buf[:, 0:HALO, :] = jnp.zeros((NSTRIP, HALO, LANES), F32)

    shift = mod_ref[0, 0:1, :]
    scale1 = 1.0 + mod_ref[0, 1:2, :]
    gate = mod_ref[0, 2:3, :]
    ng = ng_ref[...]

    for rs in row_blocks:
        xb = x_ref[0, rs, :]
        ms = jnp.mean(xb * xb, axis=-1, keepdims=True)
        y = xb * lax.rsqrt(ms + EPS) * ng
        u_buf[rs, :] = (y * scale1 + shift).astype(BF16)

    def glu_and_shift(j, pab):
        pa_all, pb_all = pab
        for e in range(STRIPS_PER_SLICE):
            s = j * STRIPS_PER_SLICE + e
            pa = pa_all[:, e * LANES:(e + 1) * LANES]
            pb = pb_all[:, e * LANES:(e + 1) * LANES]
            a_buf[s, HALO:HALO + TS, :] = pa * _sigmoid(pb)
            for res in range(1, SUBLANES):
                sh_buf[s, res - 1, :, :] = a_buf[s, res:res + SH_ROWS, :]

    u_all = u_buf[...]

    def proj(c0, width):
        return jnp.dot(u_all, win_ref[:, c0:c0 + width], preferred_element_type=F32)

    def proj_ab(j):
        return proj(COL_A + j * MXU_N, MXU_N), proj(COL_B + j * MXU_N, MXU_N)

    pab_prev = proj_ab(0)
    for j in range(1, NSLICE):
        pab_next = proj_ab(j)
        glu_and_shift(j - 1, pab_prev)
        pab_prev = pab_next
    glu_and_shift(NSLICE - 1, pab_prev)

    def zero_from(v):
        bits = pltpu.bitcast(v, jnp.uint32)
        sh16 = jnp.uint32(16)
        return pltpu.bitcast(lax.shift_right_logical(lax.shift_right_logical(bits, sh16), sh16), F32)

    dyn_zero = jnp.minimum(t, 0)
    nrep = CRB // SUBLANES

    def conv_strip(s_static, token):
        s = s_static + dyn_zero
        for r in range(0, TS, CRB):
            z = zero_from(token)
            cb8 = jnp.broadcast_to(cb_ref[s], (SUBLANES, LANES)) + z
            acc = jnp.concatenate([cb8] * nrep, axis=0)
            for res in range(SUBLANES):
                taps = [(k, off // SUBLANES) for k, off in enumerate(TAP_OFF)
                        if off % SUBLANES == res]
                span = CRB + SUBLANES * max(q for _, q in taps)
                if res == 0:
                    win = a_buf[s, r:r + span, :]
                else:
                    win = sh_buf[s, res - 1, r:r + span, :]
                for k, q in taps:
                    wk = jnp.broadcast_to(cw_ref[s, k:k + 1, :], (SUBLANES, LANES)) + z
                    acc = acc + (win[q * SUBLANES:q * SUBLANES + CRB, :]
                                 * jnp.concatenate([wk] * nrep, axis=0))
            conv_buf[s, r:r + CRB, :] = acc
            token = acc[0:SUBLANES, :]
        return token

    token = x_ref[0, 0:SUBLANES, 0:LANES]
    for g in range(NGROUP_LOOP):
        prest_buf[g] = proj(GROUP_COL[g], GROUP_W)
        for e in range(STRIPS_PER_GROUP):
            token = conv_strip(g * STRIPS_PER_GROUP + e, token)
    prest_buf[G_RG] = proj(GROUP_COL[G_RG], GROUP_W)
    a_buf[:, 0:HALO, :] = a_buf[:, TS:TS + HALO, :]

    lng = lng_ref[...]
    lnb = lnb_ref[...]
    for rs in row_blocks:
        cv = jnp.concatenate([conv_buf[s, rs, :] for s in range(NSTRIP)], axis=-1)
        mu = jnp.mean(cv, axis=-1, keepdims=True)
        cen = cv - mu
        var = jnp.mean(cen * cen, axis=-1, keepdims=True)
        ln = cen * lax.rsqrt(var + EPS) * lng + lnb
        c_buf[rs, :] = _silu(ln).astype(BF16)

    pwo = jnp.dot(c_buf[...], pw_ref[...], preferred_element_type=F32)
    for rs in row_blocks:
        ycat_buf[rs, 0:D_CONV] = (pwo[rs, :] * _silu(prest_buf[G_GATE, rs, :])).astype(BF16)

    half = DK // 2
    kscale = DK ** -0.5
    for rs in row_blocks:
        cs = cos_ref[rs, :]
        sn = sin_ref[rs, :]
        for h in range(HEADS):
            lo = slice(h * DK, h * DK + half)
            hi = slice(h * DK + half, (h + 1) * DK)
            q1 = prest_buf[G_Q, rs, lo]
            q2 = prest_buf[G_Q, rs, hi]
            q_buf[rs, lo] = (q1 * cs - q2 * sn).astype(BF16)
            q_buf[rs, hi] = (q1 * sn + q2 * cs).astype(BF16)
            k1 = prest_buf[G_K, rs, lo]
            k2 = prest_buf[G_K, rs, hi]
            o1 = (k1 * cs - k2 * sn) * kscale
            o2 = (k1 * sn + k2 * cs) * kscale
            k_buf[rs, lo] = o1.astype(BF16)
            k_buf[rs, hi] = o2.astype(BF16)
            kdr = kd_ref[h, rs, :]
            kdec_buf[rs, lo] = (o1 * kdr).astype(BF16)
            kdec_buf[rs, hi] = (o2 * kdr).astype(BF16)
        v_buf[rs, :] = prest_buf[G_V, rs, :].astype(BF16)

    for h in range(HEADS):
        hs = slice(h * DK, (h + 1) * DK)
        q = q_buf[:, hs]
        v = v_buf[:, hs]
        st = state[h]
        scores = lax.dot_general(q, k_buf[:, hs], (((1,), (1,)), ((), ())),
                                 preferred_element_type=F32) * mask_ref[h]
        inner = jnp.dot(scores.astype(BF16), v, preferred_element_type=F32)
        qd = qd_ref[h]
        cross = jnp.dot(q, st.astype(BF16), preferred_element_type=F32)
        upd = lax.dot_general(kdec_buf[:, hs], v, (((0,), (0,)), ((), ())),
                              preferred_element_type=F32)
        state[h] = st * cd_ref[h] + upd
        o = inner + cross * jnp.concatenate([qd, qd], axis=-1)
        mu = jnp.mean(o, axis=-1, keepdims=True)
        cen = o - mu
        var = jnp.mean(cen * cen, axis=-1, keepdims=True)
        gn = cen * lax.rsqrt(var + EPS) * gng_ref[:, hs] + gnb_ref[:, hs]
        rg = prest_buf[G_RG, :, hs]
        ycat_buf[:, D_CONV + h * DV:D_CONV + (h + 1) * DV] = (gn * _silu(rg)).astype(BF16)

    yo = jnp.dot(ycat_buf[...], wout_ref[...], preferred_element_type=F32)
    fg = fg_ref[...]
    for rs in row_blocks:
        hres = x_ref[0, rs, :] + gate * yo[rs, :]
        ms = jnp.mean(hres * hres, axis=-1, keepdims=True)
        out_ref[0, rs, :] = hres * lax.rsqrt(ms + EPS) * fg


def _retention_tables():
    log_g = np.log(1.0 - np.exp2(-5.0 - np.arange(HEADS, dtype=np.float64)))
    idx = np.arange(TS, dtype=np.float64)
    diff = idx[:, None] - idx[None, :]
    mask = np.where(diff >= 0, np.exp(log_g[:, None, None] * np.maximum(diff, 0.0)[None]), 0.0)
    qd = np.exp(log_g[:, None] * (idx + 1.0)[None, :])
    kd = np.exp(log_g[:, None] * (TS - 1.0 - idx)[None, :])
    cd = np.exp(log_g * TS)
    qd = np.broadcast_to(qd[:, :, None], (HEADS, TS, DV // 2))
    kd = np.broadcast_to(kd[:, :, None], (HEADS, TS, DK // 2))
    cd = np.broadcast_to(cd[:, None, None], (HEADS, 1, DV))
    return tuple(jnp.asarray(np.ascontiguousarray(a), dtype=F32) for a in (mask, qd, kd, cd))


def _rope_tables(seq):
    inv_freq = 1.0 / (ROPE_BASE ** np.linspace(0.0, 1.0, DK // 2, dtype=np.float64))
    theta = np.arange(seq, dtype=np.float64)[:, None] * inv_freq[None, :]
    return jnp.asarray(np.cos(theta), dtype=F32), jnp.asarray(np.sin(theta), dtype=F32)


def _const_spec(shape):
    nd = len(shape)
    return pl.BlockSpec(shape, lambda b, t: (0,) * nd, pipeline_mode=pl.Buffered(1))


def kernel(x, c, ada_w, ada_b, norm_g, w_in, conv_w, conv_b, conv_ln_g, conv_ln_b,
           conv_pw, ret_gn_g, ret_gn_b, w_out, final_g):
    bsz, seq, _ = x.shape
    assert seq % TS == 0 and TS % RB == 0 and TS % CRB == 0

    mod = _adaln_mod(c, ada_w[0], ada_b[0]).reshape(bsz, 3, D_MODEL)
    cos, sin = _rope_tables(seq)
    mask, qd, kd, cd = _retention_tables()
    cw = conv_w[0].reshape(CONV_W, NSTRIP, LANES).transpose(1, 0, 2)
    cb = conv_b[0].reshape(NSTRIP, 1, LANES)
    row = lambda a: a.reshape(1, -1).astype(F32)

    operands = [
        (x, pl.BlockSpec((1, TS, D_MODEL), lambda b, t: (b, t, 0))),
        (mod, pl.BlockSpec((1, 3, D_MODEL), lambda b, t: (b, 0, 0))),
        (row(norm_g[0]), _const_spec((1, D_MODEL))),
        (w_in[0].astype(BF16), _const_spec((D_MODEL, N_IN))),
        (cw, _const_spec((NSTRIP, CONV_W, LANES))),
        (cb, _const_spec((NSTRIP, 1, LANES))),
        (row(conv_ln_g[0]), _const_spec((1, D_CONV))),
        (row(conv_ln_b[0]), _const_spec((1, D_CONV))),
        (conv_pw[0].astype(BF16), _const_spec((D_CONV, D_CONV))),
        (row(ret_gn_g[0]), _const_spec((1, D_RET))),
        (row(ret_gn_b[0]), _const_spec((1, D_RET))),
        (w_out[0].astype(BF16), _const_spec((D_CONV + D_RET, D_MODEL))),
        (row(final_g), _const_spec((1, D_MODEL))),
        (cos, pl.BlockSpec((TS, DK // 2), lambda b, t: (t, 0))),
        (sin, pl.BlockSpec((TS, DK // 2), lambda b, t: (t, 0))),
        (mask, _const_spec((HEADS, TS, TS))),
        (qd, _const_spec((HEADS, TS, DV // 2))),
        (kd, _const_spec((HEADS, TS, DK // 2))),
        (cd, _const_spec((HEADS, 1, DV))),
    ]
    arrays = [a for a, _ in operands]
    in_specs = [s for _, s in operands]

    scratch = [
        pltpu.VMEM((TS, D_MODEL), BF16),
        pltpu.VMEM((NSTRIP, TS + HALO, LANES), F32),
        pltpu.VMEM((NSTRIP, SUBLANES - 1, SH_ROWS, LANES), F32),
        pltpu.VMEM((NGROUP, TS, GROUP_W), F32),
        pltpu.VMEM((NSTRIP, TS, LANES), F32),
        pltpu.VMEM((TS, D_CONV), BF16),
        pltpu.VMEM((TS, HEADS * DK), BF16),
        pltpu.VMEM((TS, HEADS * DK), BF16),
        pltpu.VMEM((TS, HEADS * DK), BF16),
        pltpu.VMEM((TS, D_RET), BF16),
        pltpu.VMEM((TS, D_CONV + D_RET), BF16),
        pltpu.VMEM((HEADS, DK, DV), F32),
    ]

    return pl.pallas_call(
        _layer_kernel,
        out_shape=jax.ShapeDtypeStruct((bsz, seq, D_MODEL), x.dtype),
        grid=(bsz, seq // TS),
        in_specs=in_specs,
        out_specs=pl.BlockSpec((1, TS, D_MODEL), lambda b, t: (b, t, 0)),
        scratch_shapes=scratch,
        compiler_params=pltpu.CompilerParams(
            dimension_semantics=("arbitrary", "arbitrary"),
            vmem_limit_bytes=VMEM_LIMIT),
        name="hybrid_layer",
    )(*arrays)
```

```python
import numpy as np
import jax
import jax.numpy as jnp
from jax import lax
from jax.experimental import pallas as pl
from jax.experimental.pallas import tpu as pltpu

D_MODEL = 1024
D_CONV = 1024
D_RET = 1024
HEADS = 4
DK = 256
DV = 256
CONV_W = 31
ROPE_BASE = 10000.0
EPS = 1e-6
N_IN = 3 * D_CONV + 2 * HEADS * DK + 2 * D_RET

SUBLANES = 8
LANES = 128
MXU_N = 256
TS = 256
HALO = 32
RB = 16
CTB = 8
MOD_TN = 512
VMEM_LIMIT = 56 * 1024 * 1024

NSLICE = D_CONV // MXU_N
GROUP_W = D_CONV
G_Q, G_K, G_V, G_GATE, G_RG = range(5)
NGROUP = 5
COL_A, COL_B = 0, D_CONV
GROUP_COL = {G_GATE: 2 * D_CONV, G_Q: 3 * D_CONV, G_K: 4 * D_CONV, G_V: 5 * D_CONV,
             G_RG: 6 * D_CONV}
NGROUP_CONV = 4
NCG = D_CONV // LANES
CG_PER_SLICE = NCG // NSLICE
assert GROUP_W == HEADS * DK == D_RET and NCG == SUBLANES

NEG_LOG2E = -1.4426950408889634

F32 = jnp.float32
BF16 = jnp.bfloat16


def _sigmoid(v):
    return 1.0 / (1.0 + jnp.exp2(v * NEG_LOG2E))


def _silu(v):
    return v * _sigmoid(v)


def _mod_kernel(c_ref, w_ref, b_ref, o_ref):
    c_act = _silu(c_ref[...])
    o_ref[...] = jnp.dot(c_act, w_ref[...], preferred_element_type=F32,
                         precision=lax.Precision.HIGHEST) + b_ref[...]


def _adaln_mod(c, ada_w, ada_b):
    bsz = c.shape[0]
    n = ada_w.shape[1]
    return pl.pallas_call(
        _mod_kernel,
        out_shape=jax.ShapeDtypeStruct((bsz, n), F32),
        grid=(n // MOD_TN,),
        in_specs=[
            pl.BlockSpec((bsz, D_MODEL), lambda j: (0, 0)),
            pl.BlockSpec((D_MODEL, MOD_TN), lambda j: (0, j)),
            pl.BlockSpec((1, MOD_TN), lambda j: (0, j)),
        ],
        out_specs=pl.BlockSpec((bsz, MOD_TN), lambda j: (0, j)),
        name="adaln_mod",
    )(c, ada_w, ada_b.reshape(1, n))


def _layer_kernel(x_ref, mod_ref, ng_ref, win_ref, cw_ref, cb_ref, lng_ref, lnb_ref,
                  pw_ref, gng_ref, gnb_ref, wout_ref, fg_ref, cos_ref, sin_ref,
                  mask_ref, qd_ref, kd_ref, cd_ref,
                  out_ref,
                  u_buf, at_buf, ct_buf, prest_buf, c_buf,
                  q_buf, kdec_buf, k_buf, v_buf, ycat_buf, state):
    t = pl.program_id(1)
    row_blocks = [slice(i * RB, (i + 1) * RB) for i in range(TS // RB)]

    @pl.when(t == 0)
    def _reset():
        state[...] = jnp.zeros_like(state)
        at_buf[0:HALO * NCG, :] = jnp.zeros((HALO * NCG, LANES), F32)

    shift = mod_ref[0, 0:1, :]
    scale1 = 1.0 + mod_ref[0, 1:2, :]
    gate = mod_ref[0, 2:3, :]
    ng = ng_ref[...]

    for rs in row_blocks:
        xb = x_ref[0, rs, :]
        ms = jnp.mean(xb * xb, axis=-1, keepdims=True)
        y = xb * lax.rsqrt(ms + EPS) * ng
        u_buf[rs, :] = (y * scale1 + shift).astype(BF16)

    u_all = u_buf[...]

    def proj(c0, width):
        return jnp.dot(u_all, win_ref[:, c0:c0 + width], preferred_element_type=F32)

    def glu_to_time_major(j, pab):
        pa_all, pb_all = pab
        for e in range(CG_PER_SLICE):
            cg = j * CG_PER_SLICE + e
            pa = pa_all[:, e * LANES:(e + 1) * LANES]
            pb = pb_all[:, e * LANES:(e + 1) * LANES]
            at_buf[pl.ds(HALO * NCG + cg, TS, stride=NCG), :] = pa * _sigmoid(pb)

    def proj_ab(j):
        return proj(COL_A + j * MXU_N, MXU_N), proj(COL_B + j * MXU_N, MXU_N)

    pab_prev = proj_ab(0)
    for j in range(1, NSLICE):
        pab_next = proj_ab(j)
        glu_to_time_major(j - 1, pab_prev)
        pab_prev = pab_next
    glu_to_time_major(NSLICE - 1, pab_prev)

    def zero_from(v):
        bits = pltpu.bitcast(v, jnp.uint32)
        sh16 = jnp.uint32(16)
        return pltpu.bitcast(lax.shift_right_logical(lax.shift_right_logical(bits, sh16), sh16), F32)

    taps = [cw_ref[k] for k in range(CONV_W)]
    bias = cb_ref[...]

    def conv_block(t0, token, token2=None):
        z = zero_from(token)
        if token2 is not None:
            z = z + zero_from(token2)
        acc = [bias + z for _ in range(CTB)]
        for tau in range(t0 - (CONV_W - 1), t0 + CTB):
            a = at_buf[(HALO + tau) * NCG:(HALO + tau + 1) * NCG, :] + z
            for i in range(CTB):
                k = tau - (t0 + i) + (CONV_W - 1)
                if 0 <= k < CONV_W:
                    acc[i] = acc[i] + a * taps[k]
        for i in range(CTB):
            ct_buf[(t0 + i) * NCG:(t0 + i + 1) * NCG, :] = acc[i]
        return acc[CTB - 1]

    blocks_per_group = TS // CTB // NGROUP_CONV
    token = x_ref[0, 0:SUBLANES, 0:LANES]
    dot_token = None
    for g in range(NGROUP_CONV):
        pd = proj(GROUP_COL[g], GROUP_W)
        prest_buf[g] = pd
        for b in range(g * blocks_per_group, (g + 1) * blocks_per_group):
            token = conv_block(b * CTB, token, dot_token)
            dot_token = None
        dot_token = pd[TS - SUBLANES:TS, GROUP_W - LANES:GROUP_W]
    prest_buf[G_RG] = proj(GROUP_COL[G_RG], GROUP_W)
    at_buf[0:HALO * NCG, :] = at_buf[TS * NCG:(TS + HALO) * NCG, :]

    lng = lng_ref[...]
    lnb = lnb_ref[...]
    for rs in row_blocks:
        cv = jnp.concatenate(
            [ct_buf[pl.ds(rs.start * NCG + cg, RB, stride=NCG), :] for cg in range(NCG)], axis=-1)
        mu = jnp.mean(cv, axis=-1, keepdims=True)
        cen = cv - mu
        var = jnp.mean(cen * cen, axis=-1, keepdims=True)
        ln = cen * lax.rsqrt(var + EPS) * lng + lnb
        c_buf[rs, :] = _silu(ln).astype(BF16)

    pwo = jnp.dot(c_buf[...], pw_ref[...], preferred_element_type=F32)
    for rs in row_blocks:
        ycat_buf[rs, 0:D_CONV] = (pwo[rs, :] * _silu(prest_buf[G_GATE, rs, :])).astype(BF16)

    half = DK // 2
    kscale = DK ** -0.5
    for rs in row_blocks:
        cs = cos_ref[rs, :]
        sn = sin_ref[rs, :]
        for h in range(HEADS):
            lo = slice(h * DK, h * DK + half)
            hi = slice(h * DK + half, (h + 1) * DK)
            q1 = prest_buf[G_Q, rs, lo]
            q2 = prest_buf[G_Q, rs, hi]
            q_buf[rs, lo] = (q1 * cs - q2 * sn).astype(BF16)
            q_buf[rs, hi] = (q1 * sn + q2 * cs).astype(BF16)
            k1 = prest_buf[G_K, rs, lo]
            k2 = prest_buf[G_K, rs, hi]
            o1 = (k1 * cs - k2 * sn) * kscale
            o2 = (k1 * sn + k2 * cs) * kscale
            k_buf[rs, lo] = o1.astype(BF16)
            k_buf[rs, hi] = o2.astype(BF16)
            kdr = kd_ref[h, rs, :]
            kdec_buf[rs, lo] = (o1 * kdr).astype(BF16)
            kdec_buf[rs, hi] = (o2 * kdr).astype(BF16)
        v_buf[rs, :] = prest_buf[G_V, rs, :].astype(BF16)

    for h in range(HEADS):
        hs = slice(h * DK, (h + 1) * DK)
        q = q_buf[:, hs]
        v = v_buf[:, hs]
        st = state[h]
        scores = lax.dot_general(q, k_buf[:, hs], (((1,), (1,)), ((), ())),
                                 preferred_element_type=F32) * mask_ref[h]
        inner = jnp.dot(scores.astype(BF16), v, preferred_element_type=F32)
        qd = qd_ref[h]
        cross = jnp.dot(q, st.astype(BF16), preferred_element_type=F32)
        upd = lax.dot_general(kdec_buf[:, hs], v, (((0,), (0,)), ((), ())),
                              preferred_element_type=F32)
        state[h] = st * cd_ref[h] + upd
        o = inner + cross * jnp.concatenate([qd, qd], axis=-1)
        mu = jnp.mean(o, axis=-1, keepdims=True)
        cen = o - mu
        var = jnp.mean(cen * cen, axis=-1, keepdims=True)
        gn = cen * lax.rsqrt(var + EPS) * gng_ref[:, hs] + gnb_ref[:, hs]
        rg = prest_buf[G_RG, :, hs]
        ycat_buf[:, D_CONV + h * DV:D_CONV + (h + 1) * DV] = (gn * _silu(rg)).astype(BF16)

    yo = jnp.dot(ycat_buf[...], wout_ref[...], preferred_element_type=F32)
    fg = fg_ref[...]
    for rs in row_blocks:
        hres = x_ref[0, rs, :] + gate * yo[rs, :]
        ms = jnp.mean(hres * hres, axis=-1, keepdims=True)
        out_ref[0, rs, :] = hres * lax.rsqrt(ms + EPS) * fg


def _retention_tables():
    log_g = np.log(1.0 - np.exp2(-5.0 - np.arange(HEADS, dtype=np.float64)))
    idx = np.arange(TS, dtype=np.float64)
    diff = idx[:, None] - idx[None, :]
    mask = np.where(diff >= 0, np.exp(log_g[:, None, None] * np.maximum(diff, 0.0)[None]), 0.0)
    qd = np.exp(log_g[:, None] * (idx + 1.0)[None, :])
    kd = np.exp(log_g[:, None] * (TS - 1.0 - idx)[None, :])
    cd = np.exp(log_g * TS)
    qd = np.broadcast_to(qd[:, :, None], (HEADS, TS, DV // 2))
    kd = np.broadcast_to(kd[:, :, None], (HEADS, TS, DK // 2))
    cd = np.broadcast_to(cd[:, None, None], (HEADS, 1, DV))
    return tuple(jnp.asarray(np.ascontiguousarray(a), dtype=F32) for a in (mask, qd, kd, cd))


def _rope_tables(seq):
    inv_freq = 1.0 / (ROPE_BASE ** np.linspace(0.0, 1.0, DK // 2, dtype=np.float64))
    theta = np.arange(seq, dtype=np.float64)[:, None] * inv_freq[None, :]
    return jnp.asarray(np.cos(theta), dtype=F32), jnp.asarray(np.sin(theta), dtype=F32)


def _const_spec(shape):
    nd = len(shape)
    return pl.BlockSpec(shape, lambda b, t: (0,) * nd, pipeline_mode=pl.Buffered(1))


def kernel(x, c, ada_w, ada_b, norm_g, w_in, conv_w, conv_b, conv_ln_g, conv_ln_b,
           conv_pw, ret_gn_g, ret_gn_b, w_out, final_g):
    bsz, seq, _ = x.shape
    assert seq % TS == 0 and TS % RB == 0 and TS % (CTB * NGROUP_CONV) == 0

    mod = _adaln_mod(c, ada_w[0], ada_b[0]).reshape(bsz, 3, D_MODEL)
    cos, sin = _rope_tables(seq)
    mask, qd, kd, cd = _retention_tables()
    cw = conv_w[0].reshape(CONV_W, NCG, LANES)
    cb = conv_b[0].reshape(NCG, LANES)
    row = lambda a: a.reshape(1, -1).astype(F32)

    operands = [
        (x, pl.BlockSpec((1, TS, D_MODEL), lambda b, t: (b, t, 0))),
        (mod, pl.BlockSpec((1, 3, D_MODEL), lambda b, t: (b, 0, 0))),
        (row(norm_g[0]), _const_spec((1, D_MODEL))),
        (w_in[0].astype(BF16), _const_spec((D_MODEL, N_IN))),
        (cw, _const_spec((CONV_W, NCG, LANES))),
        (cb, _const_spec((NCG, LANES))),
        (row(conv_ln_g[0]), _const_spec((1, D_CONV))),
        (row(conv_ln_b[0]), _const_spec((1, D_CONV))),
        (conv_pw[0].astype(BF16), _const_spec((D_CONV, D_CONV))),
        (row(ret_gn_g[0]), _const_spec((1, D_RET))),
        (row(ret_gn_b[0]), _const_spec((1, D_RET))),
        (w_out[0].astype(BF16), _const_spec((D_CONV + D_RET, D_MODEL))),
        (row(final_g), _const_spec((1, D_MODEL))),
        (cos, pl.BlockSpec((TS, DK // 2), lambda b, t: (t, 0))),
        (sin, pl.BlockSpec((TS, DK // 2), lambda b, t: (t, 0))),
        (mask, _const_spec((HEADS, TS, TS))),
        (qd, _const_spec((HEADS, TS, DV // 2))),
        (kd, _const_spec((HEADS, TS, DK // 2))),
        (cd, _const_spec((HEADS, 1, DV))),
    ]
    arrays = [a for a, _ in operands]
    in_specs = [s for _, s in operands]

    scratch = [
        pltpu.VMEM((TS, D_MODEL), BF16),
        pltpu.VMEM(((TS + HALO) * NCG, LANES), F32),
        pltpu.VMEM((TS * NCG, LANES), F32),
        pltpu.VMEM((NGROUP, TS, GROUP_W), F32),
        pltpu.VMEM((TS, D_CONV), BF16),
        pltpu.VMEM((TS, HEADS * DK), BF16),
        pltpu.VMEM((TS, HEADS * DK), BF16),
        pltpu.VMEM((TS, HEADS * DK), BF16),
        pltpu.VMEM((TS, D_RET), BF16),
        pltpu.VMEM((TS, D_CONV + D_RET), BF16),
        pltpu.VMEM((HEADS, DK, DV), F32),
    ]

    return pl.pallas_call(
        _layer_kernel,
        out_shape=jax.ShapeDtypeStruct((bsz, seq, D_MODEL), x.dtype),
        grid=(bsz, seq // TS),
        in_specs=in_specs,
        out_specs=pl.BlockSpec((1, TS, D_MODEL), lambda b, t: (b, t, 0)),
        scratch_shapes=scratch,
        compiler_params=pltpu.CompilerParams(
            dimension_semantics=("arbitrary", "arbitrary"),
            vmem_limit_bytes=VMEM_LIMIT),
        name="hybrid_layer",
    )(*arrays)
```

```python
import numpy as np
import jax
import jax.numpy as jnp
from jax import lax
from jax.experimental import pallas as pl
from jax.experimental.pallas import tpu as pltpu

D_MODEL = 1024
D_CONV = 1024
D_RET = 1024
HEADS = 4
DK = 256
DV = 256
CONV_W = 31
ROPE_BASE = 10000.0
EPS = 1e-6
N_IN = 3 * D_CONV + 2 * HEADS * DK + 2 * D_RET

SUBLANES = 8
LANES = 128
MXU_N = 256
TS = 256
HALO = 32
RB = 16
CRB = 32
TAP_GATE_EVERY = 2
MOD_TN = 512
VMEM_LIMIT = 56 * 1024 * 1024

NSLICE = D_CONV // MXU_N
GROUP_W = D_CONV
G_Q, G_K, G_V, G_GATE, G_RG = range(5)
NGROUP = 5
COL_A, COL_B = 0, D_CONV
GROUP_COL = {G_GATE: 2 * D_CONV, G_Q: 3 * D_CONV, G_K: 4 * D_CONV, G_V: 5 * D_CONV,
             G_RG: 6 * D_CONV}
NGROUP_LOOP = 4
NSTRIP = D_CONV // LANES
STRIPS_PER_SLICE = NSTRIP // NSLICE
STRIPS_PER_GROUP = NSTRIP // NGROUP_LOOP
assert GROUP_W == HEADS * DK == D_RET

TAP_OFF = [k + HALO - (CONV_W - 1) for k in range(CONV_W)]
SH_ROWS = TS + HALO - SUBLANES

NEG_LOG2E = -1.4426950408889634

F32 = jnp.float32
BF16 = jnp.bfloat16


def _sigmoid(v):
    return 1.0 / (1.0 + jnp.exp2(v * NEG_LOG2E))


def _silu(v):
    return v * _sigmoid(v)


def _mod_kernel(c_ref, w_ref, b_ref, o_ref):
    c_act = _silu(c_ref[...])
    o_ref[...] = jnp.dot(c_act, w_ref[...], preferred_element_type=F32,
                         precision=lax.Precision.HIGHEST) + b_ref[...]


def _adaln_mod(c, ada_w, ada_b):
    bsz = c.shape[0]
    n = ada_w.shape[1]
    return pl.pallas_call(
        _mod_kernel,
        out_shape=jax.ShapeDtypeStruct((bsz, n), F32),
        grid=(n // MOD_TN,),
        in_specs=[
            pl.BlockSpec((bsz, D_MODEL), lambda j: (0, 0)),
            pl.BlockSpec((D_MODEL, MOD_TN), lambda j: (0, j)),
            pl.BlockSpec((1, MOD_TN), lambda j: (0, j)),
        ],
        out_specs=pl.BlockSpec((bsz, MOD_TN), lambda j: (0, j)),
        name="adaln_mod",
    )(c, ada_w, ada_b.reshape(1, n))


def _layer_kernel(x_ref, mod_ref, ng_ref, win_ref, cw_ref, cb_ref, lng_ref, lnb_ref,
                  pw_ref, gng_ref, gnb_ref, wout_ref, fg_ref, cos_ref, sin_ref,
                  mask_ref, qd_ref, kd_ref, cd_ref,
                  out_ref,
                  u_buf, a_buf, sh_buf, prest_buf, conv_buf, c_buf,
                  q_buf, kdec_buf, k_buf, v_buf, ycat_buf, state):
    t = pl.program_id(1)
    row_blocks = [slice(i * RB, (i + 1) * RB) for i in range(TS // RB)]

    @pl.when(t == 0)
    def _reset():
        state[...] = jnp.zeros_like(state)
        a_buf[:, 0:HALO, :] = jnp.zeros((NSTRIP, HALO, LANES), F32)

    shift = mod_ref[0, 0:1, :]
    gain = ng_ref[...] * (1.0 + mod_ref[0, 1:2, :])
    gate = mod_ref[0, 2:3, :]

    for rs in row_blocks:
        xb = x_ref[0, rs, :]
        ms = jnp.mean(xb * xb, axis=-1, keepdims=True)
        u_buf[rs, :] = (xb * lax.rsqrt(ms + EPS) * gain + shift).astype(BF16)

    def glu_and_shift(j, pab):
        pa_all, pb_all = pab
        for e in range(STRIPS_PER_SLICE):
            s = j * STRIPS_PER_SLICE + e
            pa = pa_all[:, e * LANES:(e + 1) * LANES]
            pb = pb_all[:, e * LANES:(e + 1) * LANES]
            a_buf[s, HALO:HALO + TS, :] = pa * _sigmoid(pb)
            for res in range(1, SUBLANES):
                sh_buf[s, res - 1, :, :] = a_buf[s, res:res + SH_ROWS, :]

    u_all = u_buf[...]

    def proj(c0, width):
        return jnp.dot(u_all, win_ref[:, c0:c0 + width], preferred_element_type=F32)

    def proj_ab(j):
        return proj(COL_A + j * MXU_N, MXU_N), proj(COL_B + j * MXU_N, MXU_N)

    pab_prev = proj_ab(0)
    for j in range(1, NSLICE):
        pab_next = proj_ab(j)
        glu_and_shift(j - 1, pab_prev)
        pab_prev = pab_next
    glu_and_shift(NSLICE - 1, pab_prev)

    def zero_from(v):
        bits = pltpu.bitcast(v, jnp.uint32)
        sh16 = jnp.uint32(16)
        return pltpu.bitcast(lax.shift_right_logical(lax.shift_right_logical(bits, sh16), sh16), F32)

    dyn_zero = jnp.minimum(t, 0)
    nrep = CRB // SUBLANES

    def conv_strip(s_static, token):
        s = s_static + dyn_zero
        wks = None
        for bi, r in enumerate(range(0, TS, CRB)):
            z = zero_from(token)
            cb8 = jnp.broadcast_to(cb_ref[s], (SUBLANES, LANES)) + z
            acc = jnp.concatenate([cb8] * nrep, axis=0)
            if bi % TAP_GATE_EVERY == 0:
                wks = [jnp.concatenate(
                    [jnp.broadcast_to(cw_ref[s, k:k + 1, :], (SUBLANES, LANES)) + z] * nrep, axis=0)
                    for k in range(CONV_W)]
            for res in range(SUBLANES):
                taps = [(k, off // SUBLANES) for k, off in enumerate(TAP_OFF)
                        if off % SUBLANES == res]
                span = CRB + SUBLANES * max(q for _, q in taps)
                if res == 0:
                    win = a_buf[s, r:r + span, :]
                else:
                    win = sh_buf[s, res - 1, r:r + span, :]
                for k, q in taps:
                    acc = acc + win[q * SUBLANES:q * SUBLANES + CRB, :] * wks[k]
            conv_buf[s, r:r + CRB, :] = acc
            token = acc[0:SUBLANES, :]
        return token

    token = x_ref[0, 0:SUBLANES, 0:LANES]
    for g in range(NGROUP_LOOP):
        prest_buf[g] = proj(GROUP_COL[g], GROUP_W)
        for e in range(STRIPS_PER_GROUP):
            token = conv_strip(g * STRIPS_PER_GROUP + e, token)
    prest_buf[G_RG] = proj(GROUP_COL[G_RG], GROUP_W)
    a_buf[:, 0:HALO, :] = a_buf[:, TS:TS + HALO, :]

    lng = lng_ref[...]
    lnb = lnb_ref[...]
    for rs in row_blocks:
        cv = jnp.concatenate([conv_buf[s, rs, :] for s in range(NSTRIP)], axis=-1)
        mu = jnp.mean(cv, axis=-1, keepdims=True)
        cen = cv - mu
        var = jnp.mean(cen * cen, axis=-1, keepdims=True)
        ln = cen * lax.rsqrt(var + EPS) * lng + lnb
        c_buf[rs, :] = _silu(ln).astype(BF16)

    pwo = jnp.dot(c_buf[...], pw_ref[...], preferred_element_type=F32)
    for rs in row_blocks:
        ycat_buf[rs, 0:D_CONV] = (pwo[rs, :] * _silu(prest_buf[G_GATE, rs, :])).astype(BF16)

    half = DK // 2
    kscale = DK ** -0.5
    for rs in row_blocks:
        cs = cos_ref[rs, :]
        sn = sin_ref[rs, :]
        for h in range(HEADS):
            lo = slice(h * DK, h * DK + half)
            hi = slice(h * DK + half, (h + 1) * DK)
            q1 = prest_buf[G_Q, rs, lo]
            q2 = prest_buf[G_Q, rs, hi]
            q_buf[rs, lo] = (q1 * cs - q2 * sn).astype(BF16)
            q_buf[rs, hi] = (q1 * sn + q2 * cs).astype(BF16)
            k1 = prest_buf[G_K, rs, lo]
            k2 = prest_buf[G_K, rs, hi]
            o1 = (k1 * cs - k2 * sn) * kscale
            o2 = (k1 * sn + k2 * cs) * kscale
            k_buf[rs, lo] = o1.astype(BF16)
            k_buf[rs, hi] = o2.astype(BF16)
            kdr = kd_ref[h, rs, :]
            kdec_buf[rs, lo] = (o1 * kdr).astype(BF16)
            kdec_buf[rs, hi] = (o2 * kdr).astype(BF16)
        v_buf[rs, :] = prest_buf[G_V, rs, :].astype(BF16)

    for h in range(HEADS):
        hs = slice(h * DK, (h + 1) * DK)
        q = q_buf[:, hs]
        v = v_buf[:, hs]
        st = state[h]
        scores = lax.dot_general(q, k_buf[:, hs], (((1,), (1,)), ((), ())),
                                 preferred_element_type=F32) * mask_ref[h]
        inner = jnp.dot(scores.astype(BF16), v, preferred_element_type=F32)
        qd = qd_ref[h]
        cross = jnp.dot(q, st.astype(BF16), preferred_element_type=F32)
        upd = lax.dot_general(kdec_buf[:, hs], v, (((0,), (0,)), ((), ())),
                              preferred_element_type=F32)
        state[h] = st * cd_ref[h] + upd
        o = inner + cross * jnp.concatenate([qd, qd], axis=-1)
        mu = jnp.mean(o, axis=-1, keepdims=True)
        cen = o - mu
        var = jnp.mean(cen * cen, axis=-1, keepdims=True)
        gn = cen * lax.rsqrt(var + EPS) * gng_ref[:, hs] + gnb_ref[:, hs]
        rg = prest_buf[G_RG, :, hs]
        ycat_buf[:, D_CONV + h * DV:D_CONV + (h + 1) * DV] = (gn * _silu(rg)).astype(BF16)

    yo = jnp.dot(ycat_buf[...], wout_ref[...], preferred_element_type=F32)
    fg = fg_ref[...]
    for rs in row_blocks:
        hres = x_ref[0, rs, :] + gate * yo[rs, :]
        ms = jnp.mean(hres * hres, axis=-1, keepdims=True)
        out_ref[0, rs, :] = hres * lax.rsqrt(ms + EPS) * fg


def _retention_tables():
    log_g = np.log(1.0 - np.exp2(-5.0 - np.arange(HEADS, dtype=np.float64)))
    idx = np.arange(TS, dtype=np.float64)
    diff = idx[:, None] - idx[None, :]
    mask = np.where(diff >= 0, np.exp(log_g[:, None, None] * np.maximum(diff, 0.0)[None]), 0.0)
    qd = np.exp(log_g[:, None] * (idx + 1.0)[None, :])
    kd = np.exp(log_g[:, None] * (TS - 1.0 - idx)[None, :])
    cd = np.exp(log_g * TS)
    qd = np.broadcast_to(qd[:, :, None], (HEADS, TS, DV // 2))
    kd = np.broadcast_to(kd[:, :, None], (HEADS, TS, DK // 2))
    cd = np.broadcast_to(cd[:, None, None], (HEADS, 1, DV))
    return tuple(jnp.asarray(np.ascontiguousarray(a), dtype=F32) for a in (mask, qd, kd, cd))


def _rope_tables(seq):
    inv_freq = 1.0 / (ROPE_BASE ** np.linspace(0.0, 1.0, DK // 2, dtype=np.float64))
    theta = np.arange(seq, dtype=np.float64)[:, None] * inv_freq[None, :]
    return jnp.asarray(np.cos(theta), dtype=F32), jnp.asarray(np.sin(theta), dtype=F32)


def _const_spec(shape):
    nd = len(shape)
    return pl.BlockSpec(shape, lambda b, t: (0,) * nd, pipeline_mode=pl.Buffered(1))


def kernel(x, c, ada_w, ada_b, norm_g, w_in, conv_w, conv_b, conv_ln_g, conv_ln_b,
           conv_pw, ret_gn_g, ret_gn_b, w_out, final_g):
    bsz, seq, _ = x.shape
    assert seq % TS == 0 and TS % RB == 0 and TS % CRB == 0

    mod = _adaln_mod(c, ada_w[0], ada_b[0]).reshape(bsz, 3, D_MODEL)
    cos, sin = _rope_tables(seq)
    mask, qd, kd, cd = _retention_tables()
    cw = conv_w[0].reshape(CONV_W, NSTRIP, LANES).transpose(1, 0, 2)
    cb = conv_b[0].reshape(NSTRIP, 1, LANES)
    row = lambda a: a.reshape(1, -1).astype(F32)

    operands = [
        (x, pl.BlockSpec((1, TS, D_MODEL), lambda b, t: (b, t, 0))),
        (mod, pl.BlockSpec((1, 3, D_MODEL), lambda b, t: (b, 0, 0))),
        (row(norm_g[0]), _const_spec((1, D_MODEL))),
        (w_in[0].astype(BF16), _const_spec((D_MODEL, N_IN))),
        (cw, _const_spec((NSTRIP, CONV_W, LANES))),
        (cb, _const_spec((NSTRIP, 1, LANES))),
        (row(conv_ln_g[0]), _const_spec((1, D_CONV))),
        (row(conv_ln_b[0]), _const_spec((1, D_CONV))),
        (conv_pw[0].astype(BF16), _const_spec((D_CONV, D_CONV))),
        (row(ret_gn_g[0]), _const_spec((1, D_RET))),
        (row(ret_gn_b[0]), _const_spec((1, D_RET))),
        (w_out[0].astype(BF16), _const_spec((D_CONV + D_RET, D_MODEL))),
        (row(final_g), _const_spec((1, D_MODEL))),
        (cos, pl.BlockSpec((TS, DK // 2), lambda b, t: (t, 0))),
        (sin, pl.BlockSpec((TS, DK // 2), lambda b, t: (t, 0))),
        (mask, _const_spec((HEADS, TS, TS))),
        (qd, _const_spec((HEADS, TS, DV // 2))),
        (kd, _const_spec((HEADS, TS, DK // 2))),
        (cd, _const_spec((HEADS, 1, DV))),
    ]
    arrays = [a for a, _ in operands]
    in_specs = [s for _, s in operands]

    scratch = [
        pltpu.VMEM((TS, D_MODEL), BF16),
        pltpu.VMEM((NSTRIP, TS + HALO, LANES), F32),
        pltpu.VMEM((NSTRIP, SUBLANES - 1, SH_ROWS, LANES), F32),
        pltpu.VMEM((NGROUP, TS, GROUP_W), F32),
        pltpu.VMEM((NSTRIP, TS, LANES), F32),
        pltpu.VMEM((TS, D_CONV), BF16),
        pltpu.VMEM((TS, HEADS * DK), BF16),
        pltpu.VMEM((TS, HEADS * DK), BF16),
        pltpu.VMEM((TS, HEADS * DK), BF16),
        pltpu.VMEM((TS, D_RET), BF16),
        pltpu.VMEM((TS, D_CONV + D_RET), BF16),
        pltpu.VMEM((HEADS, DK, DV), F32),
    ]

    return pl.pallas_call(
        _layer_kernel,
        out_shape=jax.ShapeDtypeStruct((bsz, seq, D_MODEL), x.dtype),
        grid=(bsz, seq // TS),
        in_specs=in_specs,
        out_specs=pl.BlockSpec((1, TS, D_MODEL), lambda b, t: (b, t, 0)),
        scratch_shapes=scratch,
        compiler_params=pltpu.CompilerParams(
            dimension_semantics=("arbitrary", "arbitrary"),
            vmem_limit_bytes=VMEM_LIMIT),
        name="hybrid_layer",
    )(*arrays)
```

```python
import numpy as np
import jax
import jax.numpy as jnp
from jax import lax
from jax.experimental import pallas as pl
from jax.experimental.pallas import tpu as pltpu

D_MODEL = 1024
D_CONV = 1024
D_RET = 1024
HEADS = 4
DK = 256
DV = 256
CONV_W = 31
ROPE_BASE = 10000.0
EPS = 1e-6
N_IN = 3 * D_CONV + 2 * HEADS * DK + 2 * D_RET

SUBLANES = 8
LANES = 128
MXU_N = 256
TS = 256
HALO = 32
RB = 16
CRB = 32
TAP_GATE_EVERY = 2
MOD_TN = 512
VMEM_LIMIT = 56 * 1024 * 1024

NSLICE = D_CONV // MXU_N
GROUP_W = D_CONV
G_Q, G_K, G_V, G_GATE, G_RG = range(5)
NGROUP = 5
COL_A, COL_B = 0, D_CONV
GROUP_COL = {G_GATE: 2 * D_CONV, G_Q: 3 * D_CONV, G_K: 4 * D_CONV, G_V: 5 * D_CONV,
             G_RG: 6 * D_CONV}
NGROUP_LOOP = 4
NSTRIP = D_CONV // LANES
STRIPS_PER_SLICE = NSTRIP // NSLICE
STRIPS_PER_GROUP = NSTRIP // NGROUP_LOOP
assert GROUP_W == HEADS * DK == D_RET

TAP_OFF = [k + HALO - (CONV_W - 1) for k in range(CONV_W)]
SH_ROWS = TS + HALO - SUBLANES

NEG_LOG2E = -1.4426950408889634

F32 = jnp.float32
BF16 = jnp.bfloat16


def _sigmoid(v):
    return 1.0 / (1.0 + jnp.exp2(v * NEG_LOG2E))


def _silu(v):
    return v * _sigmoid(v)


def _mod_kernel(c_ref, w_ref, b_ref, o_ref):
    c_act = _silu(c_ref[...])
    o_ref[...] = jnp.dot(c_act, w_ref[...], preferred_element_type=F32,
                         precision=lax.Precision.HIGHEST) + b_ref[...]


def _adaln_mod(c, ada_w, ada_b):
    bsz = c.shape[0]
    n = ada_w.shape[1]
    return pl.pallas_call(
        _mod_kernel,
        out_shape=jax.ShapeDtypeStruct((bsz, n), F32),
        grid=(n // MOD_TN,),
        in_specs=[
            pl.BlockSpec((bsz, D_MODEL), lambda j: (0, 0)),
            pl.BlockSpec((D_MODEL, MOD_TN), lambda j: (0, j)),
            pl.BlockSpec((1, MOD_TN), lambda j: (0, j)),
        ],
        out_specs=pl.BlockSpec((bsz, MOD_TN), lambda j: (0, j)),
        name="adaln_mod",
    )(c, ada_w, ada_b.reshape(1, n))


def _layer_kernel(x_ref, mod_ref, ng_ref, win_hbm, cw_ref, cb_ref, lng_ref, lnb_ref,
                  pw_hbm, gng_ref, gnb_ref, wout_hbm, fg_ref, cos_ref, sin_ref,
                  mask_ref, qd_ref, kd_ref, cd_ref,
                  out_ref,
                  u_buf, a_buf, sh_buf, prest_buf, conv_buf, c_buf,
                  q_buf, kdec_buf, k_buf, v_buf, ycat_buf, state,
                  win_ref, pw_ref, wout_ref, wsem):
    t = pl.program_id(1)
    row_blocks = [slice(i * RB, (i + 1) * RB) for i in range(TS // RB)]

    def weight_copies():
        return [pltpu.make_async_copy(src, dst, wsem.at[i]) for i, (src, dst) in enumerate(
            ((win_hbm, win_ref), (pw_hbm, pw_ref), (wout_hbm, wout_ref)))]

    first_step = (t == 0) & (pl.program_id(0) == 0)

    @pl.when(first_step)
    def _start_weight_copies():
        for cp in weight_copies():
            cp.start()

    @pl.when(t == 0)
    def _reset():
        state[...] = jnp.zeros_like(state)
        a_buf[:, 0:HALO, :] = jnp.zeros((NSTRIP, HALO, LANES), F32)

    shift = mod_ref[0, 0:1, :]
    gain = ng_ref[...] * (1.0 + mod_ref[0, 1:2, :])
    gate = mod_ref[0, 2:3, :]

    for rs in row_blocks:
        xb = x_ref[0, rs, :]
        ms = jnp.mean(xb * xb, axis=-1, keepdims=True)
        u_buf[rs, :] = (xb * lax.rsqrt(ms + EPS) * gain + shift).astype(BF16)

    def glu_and_shift(j, pab):
        pa_all, pb_all = pab
        for e in range(STRIPS_PER_SLICE):
            s = j * STRIPS_PER_SLICE + e
            pa = pa_all[:, e * LANES:(e + 1) * LANES]
            pb = pb_all[:, e * LANES:(e + 1) * LANES]
            a_buf[s, HALO:HALO + TS, :] = pa * _sigmoid(pb)
            for res in range(1, SUBLANES):
                sh_buf[s, res - 1, :, :] = a_buf[s, res:res + SH_ROWS, :]

    u_all = u_buf[...]

    @pl.when(first_step)
    def _wait_weight_copies():
        for cp in weight_copies():
            cp.wait()

    def proj(c0, width):
        return jnp.dot(u_all, win_ref[:, c0:c0 + width], preferred_element_type=F32)

    def proj_ab(j):
        return proj(COL_A + j * MXU_N, MXU_N), proj(COL_B + j * MXU_N, MXU_N)

    pab_prev = proj_ab(0)
    for j in range(1, NSLICE):
        pab_next = proj_ab(j)
        glu_and_shift(j - 1, pab_prev)
        pab_prev = pab_next
    glu_and_shift(NSLICE - 1, pab_prev)

    def zero_from(v):
        bits = pltpu.bitcast(v, jnp.uint32)
        sh16 = jnp.uint32(16)
        return pltpu.bitcast(lax.shift_right_logical(lax.shift_right_logical(bits, sh16), sh16), F32)

    dyn_zero = jnp.minimum(t, 0)
    nrep = CRB // SUBLANES

    def conv_strip(s_static, token):
        s = s_static + dyn_zero
        wks = None
        for bi, r in enumerate(range(0, TS, CRB)):
            z = zero_from(token)
            cb8 = jnp.broadcast_to(cb_ref[s], (SUBLANES, LANES)) + z
            acc = jnp.concatenate([cb8] * nrep, axis=0)
            if bi % TAP_GATE_EVERY == 0:
                wks = [jnp.concatenate(
                    [jnp.broadcast_to(cw_ref[s, k:k + 1, :], (SUBLANES, LANES)) + z] * nrep, axis=0)
                    for k in range(CONV_W)]
            for res in range(SUBLANES):
                taps = [(k, off // SUBLANES) for k, off in enumerate(TAP_OFF)
                        if off % SUBLANES == res]
                span = CRB + SUBLANES * max(q for _, q in taps)
                if res == 0:
                    win = a_buf[s, r:r + span, :]
                else:
                    win = sh_buf[s, res - 1, r:r + span, :]
                for k, q in taps:
                    acc = acc + win[q * SUBLANES:q * SUBLANES + CRB, :] * wks[k]
            conv_buf[s, r:r + CRB, :] = acc
            token = acc[0:SUBLANES, :]
        return token

    token = x_ref[0, 0:SUBLANES, 0:LANES]
    for g in range(NGROUP_LOOP):
        prest_buf[g] = proj(GROUP_COL[g], GROUP_W)
        for e in range(STRIPS_PER_GROUP):
            token = conv_strip(g * STRIPS_PER_GROUP + e, token)
    prest_buf[G_RG] = proj(GROUP_COL[G_RG], GROUP_W)
    a_buf[:, 0:HALO, :] = a_buf[:, TS:TS + HALO, :]

    lng = lng_ref[...]
    lnb = lnb_ref[...]
    for rs in row_blocks:
        cv = jnp.concatenate([conv_buf[s, rs, :] for s in range(NSTRIP)], axis=-1)
        mu = jnp.mean(cv, axis=-1, keepdims=True)
        cen = cv - mu
        var = jnp.mean(cen * cen, axis=-1, keepdims=True)
        ln = cen * lax.rsqrt(var + EPS) * lng + lnb
        c_buf[rs, :] = _silu(ln).astype(BF16)

    pwo = jnp.dot(c_buf[...], pw_ref[...], preferred_element_type=F32)
    for rs in row_blocks:
        ycat_buf[rs, 0:D_CONV] = (pwo[rs, :] * _silu(prest_buf[G_GATE, rs, :])).astype(BF16)

    half = DK // 2
    kscale = DK ** -0.5
    for rs in row_blocks:
        cs = cos_ref[rs, :]
        sn = sin_ref[rs, :]
        for h in range(HEADS):
            lo = slice(h * DK, h * DK + half)
            hi = slice(h * DK + half, (h + 1) * DK)
            q1 = prest_buf[G_Q, rs, lo]
            q2 = prest_buf[G_Q, rs, hi]
            q_buf[rs, lo] = (q1 * cs - q2 * sn).astype(BF16)
            q_buf[rs, hi] = (q1 * sn + q2 * cs).astype(BF16)
            k1 = prest_buf[G_K, rs, lo]
            k2 = prest_buf[G_K, rs, hi]
            o1 = (k1 * cs - k2 * sn) * kscale
            o2 = (k1 * sn + k2 * cs) * kscale
            k_buf[rs, lo] = o1.astype(BF16)
            k_buf[rs, hi] = o2.astype(BF16)
            kdr = kd_ref[h, rs, :]
            kdec_buf[rs, lo] = (o1 * kdr).astype(BF16)
            kdec_buf[rs, hi] = (o2 * kdr).astype(BF16)
        v_buf[rs, :] = prest_buf[G_V, rs, :].astype(BF16)

    for h in range(HEADS):
        hs = slice(h * DK, (h + 1) * DK)
        q = q_buf[:, hs]
        v = v_buf[:, hs]
        st = state[h]
        scores = lax.dot_general(q, k_buf[:, hs], (((1,), (1,)), ((), ())),
                                 preferred_element_type=F32) * mask_ref[h]
        inner = jnp.dot(scores.astype(BF16), v, preferred_element_type=F32)
        qd = qd_ref[h]
        cross = jnp.dot(q, st.astype(BF16), preferred_element_type=F32)
        upd = lax.dot_general(kdec_buf[:, hs], v, (((0,), (0,)), ((), ())),
                              preferred_element_type=F32)
        state[h] = st * cd_ref[h] + upd
        o = inner + cross * jnp.concatenate([qd, qd], axis=-1)
        mu = jnp.mean(o, axis=-1, keepdims=True)
        cen = o - mu
        var = jnp.mean(cen * cen, axis=-1, keepdims=True)
        gn = cen * lax.rsqrt(var + EPS) * gng_ref[:, hs] + gnb_ref[:, hs]
        rg = prest_buf[G_RG, :, hs]
        ycat_buf[:, D_CONV + h * DV:D_CONV + (h + 1) * DV] = (gn * _silu(rg)).astype(BF16)

    yo = jnp.dot(ycat_buf[...], wout_ref[...], preferred_element_type=F32)
    fg = fg_ref[...]
    for rs in row_blocks:
        hres = x_ref[0, rs, :] + gate * yo[rs, :]
        ms = jnp.mean(hres * hres, axis=-1, keepdims=True)
        out_ref[0, rs, :] = hres * lax.rsqrt(ms + EPS) * fg


def _retention_tables():
    log_g = np.log(1.0 - np.exp2(-5.0 - np.arange(HEADS, dtype=np.float64)))
    idx = np.arange(TS, dtype=np.float64)
    diff = idx[:, None] - idx[None, :]
    mask = np.where(diff >= 0, np.exp(log_g[:, None, None] * np.maximum(diff, 0.0)[None]), 0.0)
    qd = np.exp(log_g[:, None] * (idx + 1.0)[None, :])
    kd = np.exp(log_g[:, None] * (TS - 1.0 - idx)[None, :])
    cd = np.exp(log_g * TS)
    qd = np.broadcast_to(qd[:, :, None], (HEADS, TS, DV // 2))
    kd = np.broadcast_to(kd[:, :, None], (HEADS, TS, DK // 2))
    cd = np.broadcast_to(cd[:, None, None], (HEADS, 1, DV))
    return tuple(jnp.asarray(np.ascontiguousarray(a), dtype=F32) for a in (mask, qd, kd, cd))


def _rope_tables(seq):
    inv_freq = 1.0 / (ROPE_BASE ** np.linspace(0.0, 1.0, DK // 2, dtype=np.float64))
    theta = np.arange(seq, dtype=np.float64)[:, None] * inv_freq[None, :]
    return jnp.asarray(np.cos(theta), dtype=F32), jnp.asarray(np.sin(theta), dtype=F32)


def _const_spec(shape):
    nd = len(shape)
    return pl.BlockSpec(shape, lambda b, t: (0,) * nd, pipeline_mode=pl.Buffered(1))


def kernel(x, c, ada_w, ada_b, norm_g, w_in, conv_w, conv_b, conv_ln_g, conv_ln_b,
           conv_pw, ret_gn_g, ret_gn_b, w_out, final_g):
    bsz, seq, _ = x.shape
    assert seq % TS == 0 and TS % RB == 0 and TS % CRB == 0

    mod = _adaln_mod(c, ada_w[0], ada_b[0]).reshape(bsz, 3, D_MODEL)
    cos, sin = _rope_tables(seq)
    mask, qd, kd, cd = _retention_tables()
    cw = conv_w[0].reshape(CONV_W, NSTRIP, LANES).transpose(1, 0, 2)
    cb = conv_b[0].reshape(NSTRIP, 1, LANES)
    row = lambda a: a.reshape(1, -1).astype(F32)

    operands = [
        (x, pl.BlockSpec((1, TS, D_MODEL), lambda b, t: (b, t, 0))),
        (mod, pl.BlockSpec((1, 3, D_MODEL), lambda b, t: (b, 0, 0))),
        (row(norm_g[0]), _const_spec((1, D_MODEL))),
        (w_in[0].astype(BF16), pl.BlockSpec(memory_space=pl.ANY)),
        (cw, _const_spec((NSTRIP, CONV_W, LANES))),
        (cb, _const_spec((NSTRIP, 1, LANES))),
        (row(conv_ln_g[0]), _const_spec((1, D_CONV))),
        (row(conv_ln_b[0]), _const_spec((1, D_CONV))),
        (conv_pw[0].astype(BF16), pl.BlockSpec(memory_space=pl.ANY)),
        (row(ret_gn_g[0]), _const_spec((1, D_RET))),
        (row(ret_gn_b[0]), _const_spec((1, D_RET))),
        (w_out[0].astype(BF16), pl.BlockSpec(memory_space=pl.ANY)),
        (row(final_g), _const_spec((1, D_MODEL))),
        (cos, pl.BlockSpec((TS, DK // 2), lambda b, t: (t, 0))),
        (sin, pl.BlockSpec((TS, DK // 2), lambda b, t: (t, 0))),
        (mask, _const_spec((HEADS, TS, TS))),
        (qd, _const_spec((HEADS, TS, DV // 2))),
        (kd, _const_spec((HEADS, TS, DK // 2))),
        (cd, _const_spec((HEADS, 1, DV))),
    ]
    arrays = [a for a, _ in operands]
    in_specs = [s for _, s in operands]

    scratch = [
        pltpu.VMEM((TS, D_MODEL), BF16),
        pltpu.VMEM((NSTRIP, TS + HALO, LANES), F32),
        pltpu.VMEM((NSTRIP, SUBLANES - 1, SH_ROWS, LANES), F32),
        pltpu.VMEM((NGROUP, TS, GROUP_W), F32),
        pltpu.VMEM((NSTRIP, TS, LANES), F32),
        pltpu.VMEM((TS, D_CONV), BF16),
        pltpu.VMEM((TS, HEADS * DK), BF16),
        pltpu.VMEM((TS, HEADS * DK), BF16),
        pltpu.VMEM((TS, HEADS * DK), BF16),
        pltpu.VMEM((TS, D_RET), BF16),
        pltpu.VMEM((TS, D_CONV + D_RET), BF16),
        pltpu.VMEM((HEADS, DK, DV), F32),
        pltpu.VMEM((D_MODEL, N_IN), BF16),
        pltpu.VMEM((D_CONV, D_CONV), BF16),
        pltpu.VMEM((D_CONV + D_RET, D_MODEL), BF16),
        pltpu.SemaphoreType.DMA((3,)),
    ]

    return pl.pallas_call(
        _layer_kernel,
        out_shape=jax.ShapeDtypeStruct((bsz, seq, D_MODEL), x.dtype),
        grid=(bsz, seq // TS),
        in_specs=in_specs,
        out_specs=pl.BlockSpec((1, TS, D_MODEL), lambda b, t: (b, t, 0)),
        scratch_shapes=scratch,
        compiler_params=pltpu.CompilerParams(
            dimension_semantics=("arbitrary", "arbitrary"),
            vmem_limit_bytes=VMEM_LIMIT),
        name="hybrid_layer",
    )(*arrays)
```

```python
import numpy as np
import jax
import jax.numpy as jnp
from jax import lax
from jax.experimental import pallas as pl
from jax.experimental.pallas import tpu as pltpu

D_MODEL = 1024
D_CONV = 1024
D_RET = 1024
HEADS = 4
DK = 256
DV = 256
CONV_W = 31
ROPE_BASE = 10000.0
EPS = 1e-6
N_IN = 3 * D_CONV + 2 * HEADS * DK + 2 * D_RET

SUBLANES = 8
LANES = 128
MXU_N = 256
TS = 256
HALO = 32
RB = 16
CRB = 32
TAP_GATE_EVERY = 2
MOD_TN = 512
STAGE_ROWS_WIDE = 64
STAGE_ROWS_SQ = 256
VMEM_LIMIT = 56 * 1024 * 1024

NSLICE = D_CONV // MXU_N
GROUP_W = D_CONV
G_Q, G_K, G_V, G_GATE, G_RG = range(5)
NGROUP = 5
COL_A, COL_B = 0, D_CONV
GROUP_COL = {G_GATE: 2 * D_CONV, G_Q: 3 * D_CONV, G_K: 4 * D_CONV, G_V: 5 * D_CONV,
             G_RG: 6 * D_CONV}
NGROUP_LOOP = 4
NSTRIP = D_CONV // LANES
STRIPS_PER_SLICE = NSTRIP // NSLICE
STRIPS_PER_GROUP = NSTRIP // NGROUP_LOOP
assert GROUP_W == HEADS * DK == D_RET

TAP_OFF = [k + HALO - (CONV_W - 1) for k in range(CONV_W)]
SH_ROWS = TS + HALO - SUBLANES

NEG_LOG2E = -1.4426950408889634

F32 = jnp.float32
BF16 = jnp.bfloat16


def _sigmoid(v):
    return 1.0 / (1.0 + jnp.exp2(v * NEG_LOG2E))


def _silu(v):
    return v * _sigmoid(v)


def _mod_kernel(c_ref, w_ref, b_ref, o_ref):
    c_act = _silu(c_ref[...])
    o_ref[...] = jnp.dot(c_act, w_ref[...], preferred_element_type=F32,
                         precision=lax.Precision.HIGHEST) + b_ref[...]


def _adaln_mod(c, ada_w, ada_b):
    bsz = c.shape[0]
    n = ada_w.shape[1]
    return pl.pallas_call(
        _mod_kernel,
        out_shape=jax.ShapeDtypeStruct((bsz, n), F32),
        grid=(n // MOD_TN,),
        in_specs=[
            pl.BlockSpec((bsz, D_MODEL), lambda j: (0, 0)),
            pl.BlockSpec((D_MODEL, MOD_TN), lambda j: (0, j)),
            pl.BlockSpec((1, MOD_TN), lambda j: (0, j)),
        ],
        out_specs=pl.BlockSpec((bsz, MOD_TN), lambda j: (0, j)),
        name="adaln_mod",
    )(c, ada_w, ada_b.reshape(1, n))


def _layer_kernel(x_ref, mod_ref, ng_ref, win_hbm, cw_ref, cb_ref, lng_ref, lnb_ref,
                  pw_hbm, gng_ref, gnb_ref, wout_hbm, fg_ref, cos_ref, sin_ref,
                  mask_ref, qd_ref, kd_ref, cd_ref,
                  out_ref,
                  u_buf, a_buf, sh_buf, prest_buf, conv_buf, c_buf,
                  q_buf, kdec_buf, k_buf, v_buf, ycat_buf, state,
                  win_ref, pw_ref, wout_ref, stage_wide, stage_sq, wsem):
    t = pl.program_id(1)
    row_blocks = [slice(i * RB, (i + 1) * RB) for i in range(TS // RB)]

    def stage_weights(src_hbm, dst_ref, stage, rows):
        nchunk = src_hbm.shape[0] // rows

        def chunk_copy(i, slot):
            return pltpu.make_async_copy(
                src_hbm.at[pl.ds(pl.multiple_of(i * rows, rows), rows), :],
                stage.at[slot], wsem.at[slot])

        chunk_copy(0, 0).start()

        def body(i, carry):
            slot = lax.rem(i, 2)
            chunk_copy(i, slot).wait()

            @pl.when(i + 1 < nchunk)
            def _prefetch():
                chunk_copy(i + 1, 1 - slot).start()

            dst_ref[pl.ds(pl.multiple_of(i * rows, rows), rows), :] = stage[slot].astype(BF16)
            return carry
        lax.fori_loop(0, nchunk, body, 0)

    @pl.when((t == 0) & (pl.program_id(0) == 0))
    def _load_weights():
        stage_weights(win_hbm, win_ref, stage_wide, STAGE_ROWS_WIDE)
        stage_weights(pw_hbm, pw_ref, stage_sq, STAGE_ROWS_SQ)
        stage_weights(wout_hbm, wout_ref, stage_sq, STAGE_ROWS_SQ)

    @pl.when(t == 0)
    def _reset():
        state[...] = jnp.zeros_like(state)
        a_buf[:, 0:HALO, :] = jnp.zeros((NSTRIP, HALO, LANES), F32)

    shift = mod_ref[0, 0:1, :]
    gain = ng_ref[...] * (1.0 + mod_ref[0, 1:2, :])
    gate = mod_ref[0, 2:3, :]

    for rs in row_blocks:
        xb = x_ref[0, rs, :]
        ms = jnp.mean(xb * xb, axis=-1, keepdims=True)
        u_buf[rs, :] = (xb * lax.rsqrt(ms + EPS) * gain + shift).astype(BF16)

    def glu_and_shift(j, pab):
        pa_all, pb_all = pab
        for e in range(STRIPS_PER_SLICE):
            s = j * STRIPS_PER_SLICE + e
            pa = pa_all[:, e * LANES:(e + 1) * LANES]
            pb = pb_all[:, e * LANES:(e + 1) * LANES]
            a_buf[s, HALO:HALO + TS, :] = pa * _sigmoid(pb)
            for res in range(1, SUBLANES):
                sh_buf[s, res - 1, :, :] = a_buf[s, res:res + SH_ROWS, :]

    u_all = u_buf[...]

    def proj(c0, width):
        return jnp.dot(u_all, win_ref[:, c0:c0 + width], preferred_element_type=F32)

    def proj_ab(j):
        return proj(COL_A + j * MXU_N, MXU_N), proj(COL_B + j * MXU_N, MXU_N)

    pab_prev = proj_ab(0)
    for j in range(1, NSLICE):
        pab_next = proj_ab(j)
        glu_and_shift(j - 1, pab_prev)
        pab_prev = pab_next
    glu_and_shift(NSLICE - 1, pab_prev)

    def zero_from(v):
        bits = pltpu.bitcast(v, jnp.uint32)
        sh16 = jnp.uint32(16)
        return pltpu.bitcast(lax.shift_right_logical(lax.shift_right_logical(bits, sh16), sh16), F32)

    dyn_zero = jnp.minimum(t, 0)
    nrep = CRB // SUBLANES

    def conv_strip(s_static, token):
        s = s_static + dyn_zero
        wks = None
        for bi, r in enumerate(range(0, TS, CRB)):
            z = zero_from(token)
            cb8 = jnp.broadcast_to(cb_ref[s], (SUBLANES, LANES)) + z
            acc = jnp.concatenate([cb8] * nrep, axis=0)
            if bi % TAP_GATE_EVERY == 0:
                wks = [jnp.concatenate(
                    [jnp.broadcast_to(cw_ref[s, k:k + 1, :], (SUBLANES, LANES)) + z] * nrep, axis=0)
                    for k in range(CONV_W)]
            for res in range(SUBLANES):
                taps = [(k, off // SUBLANES) for k, off in enumerate(TAP_OFF)
                        if off % SUBLANES == res]
                span = CRB + SUBLANES * max(q for _, q in taps)
                if res == 0:
                    win = a_buf[s, r:r + span, :]
                else:
                    win = sh_buf[s, res - 1, r:r + span, :]
                for k, q in taps:
                    acc = acc + win[q * SUBLANES:q * SUBLANES + CRB, :] * wks[k]
            conv_buf[s, r:r + CRB, :] = acc
            token = acc[0:SUBLANES, :]
        return token

    token = x_ref[0, 0:SUBLANES, 0:LANES]
    for g in range(NGROUP_LOOP):
        prest_buf[g] = proj(GROUP_COL[g], GROUP_W)
        for e in range(STRIPS_PER_GROUP):
            token = conv_strip(g * STRIPS_PER_GROUP + e, token)
    prest_buf[G_RG] = proj(GROUP_COL[G_RG], GROUP_W)
    a_buf[:, 0:HALO, :] = a_buf[:, TS:TS + HALO, :]

    lng = lng_ref[...]
    lnb = lnb_ref[...]
    for rs in row_blocks:
        cv = jnp.concatenate([conv_buf[s, rs, :] for s in range(NSTRIP)], axis=-1)
        mu = jnp.mean(cv, axis=-1, keepdims=True)
        cen = cv - mu
        var = jnp.mean(cen * cen, axis=-1, keepdims=True)
        ln = cen * lax.rsqrt(var + EPS) * lng + lnb
        c_buf[rs, :] = _silu(ln).astype(BF16)

    pwo = jnp.dot(c_buf[...], pw_ref[...], preferred_element_type=F32)
    for rs in row_blocks:
        ycat_buf[rs, 0:D_CONV] = (pwo[rs, :] * _silu(prest_buf[G_GATE, rs, :])).astype(BF16)

    half = DK // 2
    kscale = DK ** -0.5
    for rs in row_blocks:
        cs = cos_ref[rs, :]
        sn = sin_ref[rs, :]
        for h in range(HEADS):
            lo = slice(h * DK, h * DK + half)
            hi = slice(h * DK + half, (h + 1) * DK)
            q1 = prest_buf[G_Q, rs, lo]
            q2 = prest_buf[G_Q, rs, hi]
            q_buf[rs, lo] = (q1 * cs - q2 * sn).astype(BF16)
            q_buf[rs, hi] = (q1 * sn + q2 * cs).astype(BF16)
            k1 = prest_buf[G_K, rs, lo]
            k2 = prest_buf[G_K, rs, hi]
            o1 = (k1 * cs - k2 * sn) * kscale
            o2 = (k1 * sn + k2 * cs) * kscale
            k_buf[rs, lo] = o1.astype(BF16)
            k_buf[rs, hi] = o2.astype(BF16)
            kdr = kd_ref[h, rs, :]
            kdec_buf[rs, lo] = (o1 * kdr).astype(BF16)
            kdec_buf[rs, hi] = (o2 * kdr).astype(BF16)
        v_buf[rs, :] = prest_buf[G_V, rs, :].astype(BF16)

    for h in range(HEADS):
        hs = slice(h * DK, (h + 1) * DK)
        q = q_buf[:, hs]
        v = v_buf[:, hs]
        st = state[h]
        scores = lax.dot_general(q, k_buf[:, hs], (((1,), (1,)), ((), ())),
                                 preferred_element_type=F32) * mask_ref[h]
        inner = jnp.dot(scores.astype(BF16), v, preferred_element_type=F32)
        qd = qd_ref[h]
        cross = jnp.dot(q, st.astype(BF16), preferred_element_type=F32)
        upd = lax.dot_general(kdec_buf[:, hs], v, (((0,), (0,)), ((), ())),
                              preferred_element_type=F32)
        state[h] = st * cd_ref[h] + upd
        o = inner + cross * jnp.concatenate([qd, qd], axis=-1)
        mu = jnp.mean(o, axis=-1, keepdims=True)
        cen = o - mu
        var = jnp.mean(cen * cen, axis=-1, keepdims=True)
        gn = cen * lax.rsqrt(var + EPS) * gng_ref[:, hs] + gnb_ref[:, hs]
        rg = prest_buf[G_RG, :, hs]
        ycat_buf[:, D_CONV + h * DV:D_CONV + (h + 1) * DV] = (gn * _silu(rg)).astype(BF16)

    yo = jnp.dot(ycat_buf[...], wout_ref[...], preferred_element_type=F32)
    fg = fg_ref[...]
    for rs in row_blocks:
        hres = x_ref[0, rs, :] + gate * yo[rs, :]
        ms = jnp.mean(hres * hres, axis=-1, keepdims=True)
        out_ref[0, rs, :] = hres * lax.rsqrt(ms + EPS) * fg


def _retention_tables():
    log_g = np.log(1.0 - np.exp2(-5.0 - np.arange(HEADS, dtype=np.float64)))
    idx = np.arange(TS, dtype=np.float64)
    diff = idx[:, None] - idx[None, :]
    mask = np.where(diff >= 0, np.exp(log_g[:, None, None] * np.maximum(diff, 0.0)[None]), 0.0)
    qd = np.exp(log_g[:, None] * (idx + 1.0)[None, :])
    kd = np.exp(log_g[:, None] * (TS - 1.0 - idx)[None, :])
    cd = np.exp(log_g * TS)
    qd = np.broadcast_to(qd[:, :, None], (HEADS, TS, DV // 2))
    kd = np.broadcast_to(kd[:, :, None], (HEADS, TS, DK // 2))
    cd = np.broadcast_to(cd[:, None, None], (HEADS, 1, DV))
    return tuple(jnp.asarray(np.ascontiguousarray(a), dtype=F32) for a in (mask, qd, kd, cd))


def _rope_tables(seq):
    inv_freq = 1.0 / (ROPE_BASE ** np.linspace(0.0, 1.0, DK // 2, dtype=np.float64))
    theta = np.arange(seq, dtype=np.float64)[:, None] * inv_freq[None, :]
    return jnp.asarray(np.cos(theta), dtype=F32), jnp.asarray(np.sin(theta), dtype=F32)


def _const_spec(shape):
    nd = len(shape)
    return pl.BlockSpec(shape, lambda b, t: (0,) * nd, pipeline_mode=pl.Buffered(1))


def kernel(x, c, ada_w, ada_b, norm_g, w_in, conv_w, conv_b, conv_ln_g, conv_ln_b,
           conv_pw, ret_gn_g, ret_gn_b, w_out, final_g):
    bsz, seq, _ = x.shape
    assert seq % TS == 0 and TS % RB == 0 and TS % CRB == 0

    mod = _adaln_mod(c, ada_w[0], ada_b[0]).reshape(bsz, 3, D_MODEL)
    cos, sin = _rope_tables(seq)
    mask, qd, kd, cd = _retention_tables()
    cw = conv_w[0].reshape(CONV_W, NSTRIP, LANES).transpose(1, 0, 2)
    cb = conv_b[0].reshape(NSTRIP, 1, LANES)
    row = lambda a: a.reshape(1, -1).astype(F32)

    operands = [
        (x, pl.BlockSpec((1, TS, D_MODEL), lambda b, t: (b, t, 0))),
        (mod, pl.BlockSpec((1, 3, D_MODEL), lambda b, t: (b, 0, 0))),
        (row(norm_g[0]), _const_spec((1, D_MODEL))),
        (w_in[0], pl.BlockSpec(memory_space=pl.ANY)),
        (cw, _const_spec((NSTRIP, CONV_W, LANES))),
        (cb, _const_spec((NSTRIP, 1, LANES))),
        (row(conv_ln_g[0]), _const_spec((1, D_CONV))),
        (row(conv_ln_b[0]), _const_spec((1, D_CONV))),
        (conv_pw[0], pl.BlockSpec(memory_space=pl.ANY)),
        (row(ret_gn_g[0]), _const_spec((1, D_RET))),
        (row(ret_gn_b[0]), _const_spec((1, D_RET))),
        (w_out[0], pl.BlockSpec(memory_space=pl.ANY)),
        (row(final_g), _const_spec((1, D_MODEL))),
        (cos, pl.BlockSpec((TS, DK // 2), lambda b, t: (t, 0))),
        (sin, pl.BlockSpec((TS, DK // 2), lambda b, t: (t, 0))),
        (mask, _const_spec((HEADS, TS, TS))),
        (qd, _const_spec((HEADS, TS, DV // 2))),
        (kd, _const_spec((HEADS, TS, DK // 2))),
        (cd, _const_spec((HEADS, 1, DV))),
    ]
    arrays = [a for a, _ in operands]
    in_specs = [s for _, s in operands]

    scratch = [
        pltpu.VMEM((TS, D_MODEL), BF16),
        pltpu.VMEM((NSTRIP, TS + HALO, LANES), F32),
        pltpu.VMEM((NSTRIP, SUBLANES - 1, SH_ROWS, LANES), F32),
        pltpu.VMEM((NGROUP, TS, GROUP_W), F32),
        pltpu.VMEM((NSTRIP, TS, LANES), F32),
        pltpu.VMEM((TS, D_CONV), BF16),
        pltpu.VMEM((TS, HEADS * DK), BF16),
        pltpu.VMEM((TS, HEADS * DK), BF16),
        pltpu.VMEM((TS, HEADS * DK), BF16),
        pltpu.VMEM((TS, D_RET), BF16),
        pltpu.VMEM((TS, D_CONV + D_RET), BF16),
        pltpu.VMEM((HEADS, DK, DV), F32),
        pltpu.VMEM((D_MODEL, N_IN), BF16),
        pltpu.VMEM((D_CONV, D_CONV), BF16),
        pltpu.VMEM((D_CONV + D_RET, D_MODEL), BF16),
        pltpu.VMEM((2, STAGE_ROWS_WIDE, N_IN), F32),
        pltpu.VMEM((2, STAGE_ROWS_SQ, D_MODEL), F32),
        pltpu.SemaphoreType.DMA((2,)),
    ]

    return pl.pallas_call(
        _layer_kernel,
        out_shape=jax.ShapeDtypeStruct((bsz, seq, D_MODEL), x.dtype),
        grid=(bsz, seq // TS),
        in_specs=in_specs,
        out_specs=pl.BlockSpec((1, TS, D_MODEL), lambda b, t: (b, t, 0)),
        scratch_shapes=scratch,
        compiler_params=pltpu.CompilerParams(
            dimension_semantics=("arbitrary", "arbitrary"),
            vmem_limit_bytes=VMEM_LIMIT),
        name="hybrid_layer",
    )(*arrays)
```

```python
import numpy as np
import jax
import jax.numpy as jnp
from jax import lax
from jax.experimental import pallas as pl
from jax.experimental.pallas import tpu as pltpu

D_MODEL = 1024
D_CONV = 1024
D_RET = 1024
HEADS = 4
DK = 256
DV = 256
CONV_W = 31
ROPE_BASE = 10000.0
EPS = 1e-6
N_IN = 3 * D_CONV + 2 * HEADS * DK + 2 * D_RET

SUBLANES = 8
LANES = 128
MXU_N = 256
TS = 256
HALO = 32
RB = 16
CRB = 32
TAP_GATE_EVERY = 2
MOD_TN = 512
STAGE_SLOTS = 8
STAGE_ROWS_WIDE = 16
STAGE_ROWS_SQ = 64
VMEM_LIMIT = 56 * 1024 * 1024

NSLICE = D_CONV // MXU_N
GROUP_W = D_CONV
G_Q, G_K, G_V, G_GATE, G_RG = range(5)
NGROUP = 5
COL_A, COL_B = 0, D_CONV
GROUP_COL = {G_GATE: 2 * D_CONV, G_Q: 3 * D_CONV, G_K: 4 * D_CONV, G_V: 5 * D_CONV,
             G_RG: 6 * D_CONV}
NGROUP_LOOP = 4
NSTRIP = D_CONV // LANES
STRIPS_PER_SLICE = NSTRIP // NSLICE
STRIPS_PER_GROUP = NSTRIP // NGROUP_LOOP
assert GROUP_W == HEADS * DK == D_RET

TAP_OFF = [k + HALO - (CONV_W - 1) for k in range(CONV_W)]
SH_ROWS = TS + HALO - SUBLANES

NEG_LOG2E = -1.4426950408889634

F32 = jnp.float32
BF16 = jnp.bfloat16


def _sigmoid(v):
    return 1.0 / (1.0 + jnp.exp2(v * NEG_LOG2E))


def _silu(v):
    return v * _sigmoid(v)


def _mod_kernel(c_ref, w_ref, b_ref, o_ref):
    c_act = _silu(c_ref[...])
    o_ref[...] = jnp.dot(c_act, w_ref[...], preferred_element_type=F32,
                         precision=lax.Precision.HIGHEST) + b_ref[...]


def _adaln_mod(c, ada_w, ada_b):
    bsz = c.shape[0]
    n = ada_w.shape[1]
    return pl.pallas_call(
        _mod_kernel,
        out_shape=jax.ShapeDtypeStruct((bsz, n), F32),
        grid=(n // MOD_TN,),
        in_specs=[
            pl.BlockSpec((bsz, D_MODEL), lambda j: (0, 0)),
            pl.BlockSpec((D_MODEL, MOD_TN), lambda j: (0, j)),
            pl.BlockSpec((1, MOD_TN), lambda j: (0, j)),
        ],
        out_specs=pl.BlockSpec((bsz, MOD_TN), lambda j: (0, j)),
        name="adaln_mod",
    )(c, ada_w, ada_b.reshape(1, n))


def _layer_kernel(x_ref, mod_ref, ng_ref, win_hbm, cw_ref, cb_ref, lng_ref, lnb_ref,
                  pw_hbm, gng_ref, gnb_ref, wout_hbm, fg_ref, cos_ref, sin_ref,
                  mask_ref, qd_ref, kd_ref, cd_ref,
                  out_ref,
                  u_buf, a_buf, sh_buf, prest_buf, conv_buf, c_buf,
                  q_buf, kdec_buf, k_buf, v_buf, ycat_buf, state,
                  win_ref, pw_ref, wout_ref, stage_wide, stage_sq, wsem):
    t = pl.program_id(1)
    row_blocks = [slice(i * RB, (i + 1) * RB) for i in range(TS // RB)]

    def stage_weights(src_hbm, dst_ref, stage, rows):
        nchunk = src_hbm.shape[0] // rows

        def chunk_copy(i, slot):
            return pltpu.make_async_copy(
                src_hbm.at[pl.ds(pl.multiple_of(i * rows, rows), rows), :],
                stage.at[slot], wsem.at[slot])

        ahead = STAGE_SLOTS - 1
        for i in range(ahead):
            chunk_copy(i, i).start()

        def body(i, carry):
            slot = lax.rem(i, STAGE_SLOTS)
            chunk_copy(i, slot).wait()

            @pl.when(i + ahead < nchunk)
            def _prefetch():
                chunk_copy(i + ahead, lax.rem(i + ahead, STAGE_SLOTS)).start()

            dst_ref[pl.ds(pl.multiple_of(i * rows, rows), rows), :] = stage[slot].astype(BF16)
            return carry
        lax.fori_loop(0, nchunk, body, 0)

    @pl.when((t == 0) & (pl.program_id(0) == 0))
    def _load_weights():
        stage_weights(win_hbm, win_ref, stage_wide, STAGE_ROWS_WIDE)
        stage_weights(pw_hbm, pw_ref, stage_sq, STAGE_ROWS_SQ)
        stage_weights(wout_hbm, wout_ref, stage_sq, STAGE_ROWS_SQ)

    @pl.when(t == 0)
    def _reset():
        state[...] = jnp.zeros_like(state)
        a_buf[:, 0:HALO, :] = jnp.zeros((NSTRIP, HALO, LANES), F32)

    shift = mod_ref[0, 0:1, :]
    gain = ng_ref[...] * (1.0 + mod_ref[0, 1:2, :])
    gate = mod_ref[0, 2:3, :]

    for rs in row_blocks:
        xb = x_ref[0, rs, :]
        ms = jnp.mean(xb * xb, axis=-1, keepdims=True)
        u_buf[rs, :] = (xb * lax.rsqrt(ms + EPS) * gain + shift).astype(BF16)

    def glu_and_shift(j, pab):
        pa_all, pb_all = pab
        for e in range(STRIPS_PER_SLICE):
            s = j * STRIPS_PER_SLICE + e
            pa = pa_all[:, e * LANES:(e + 1) * LANES]
            pb = pb_all[:, e * LANES:(e + 1) * LANES]
            a_buf[s, HALO:HALO + TS, :] = pa * _sigmoid(pb)
            for res in range(1, SUBLANES):
                sh_buf[s, res - 1, :, :] = a_buf[s, res:res + SH_ROWS, :]

    u_all = u_buf[...]

    def proj(c0, width):
        return jnp.dot(u_all, win_ref[:, c0:c0 + width], preferred_element_type=F32)

    def proj_ab(j):
        return proj(COL_A + j * MXU_N, MXU_N), proj(COL_B + j * MXU_N, MXU_N)

    pab_prev = proj_ab(0)
    for j in range(1, NSLICE):
        pab_next = proj_ab(j)
        glu_and_shift(j - 1, pab_prev)
        pab_prev = pab_next
    glu_and_shift(NSLICE - 1, pab_prev)

    def zero_from(v):
        bits = pltpu.bitcast(v, jnp.uint32)
        sh16 = jnp.uint32(16)
        return pltpu.bitcast(lax.shift_right_logical(lax.shift_right_logical(bits, sh16), sh16), F32)

    dyn_zero = jnp.minimum(t, 0)
    nrep = CRB // SUBLANES

    def conv_strip(s_static, token):
        s = s_static + dyn_zero
        wks = None
        for bi, r in enumerate(range(0, TS, CRB)):
            z = zero_from(token)
            cb8 = jnp.broadcast_to(cb_ref[s], (SUBLANES, LANES)) + z
            acc = jnp.concatenate([cb8] * nrep, axis=0)
            if bi % TAP_GATE_EVERY == 0:
                wks = [jnp.concatenate(
                    [jnp.broadcast_to(cw_ref[s, k:k + 1, :], (SUBLANES, LANES)) + z] * nrep, axis=0)
                    for k in range(CONV_W)]
            for res in range(SUBLANES):
                taps = [(k, off // SUBLANES) for k, off in enumerate(TAP_OFF)
                        if off % SUBLANES == res]
                span = CRB + SUBLANES * max(q for _, q in taps)
                if res == 0:
                    win = a_buf[s, r:r + span, :]
                else:
                    win = sh_buf[s, res - 1, r:r + span, :]
                for k, q in taps:
                    acc = acc + win[q * SUBLANES:q * SUBLANES + CRB, :] * wks[k]
            conv_buf[s, r:r + CRB, :] = acc
            token = acc[0:SUBLANES, :]
        return token

    token = x_ref[0, 0:SUBLANES, 0:LANES]
    for g in range(NGROUP_LOOP):
        prest_buf[g] = proj(GROUP_COL[g], GROUP_W)
        for e in range(STRIPS_PER_GROUP):
            token = conv_strip(g * STRIPS_PER_GROUP + e, token)
    prest_buf[G_RG] = proj(GROUP_COL[G_RG], GROUP_W)
    a_buf[:, 0:HALO, :] = a_buf[:, TS:TS + HALO, :]

    lng = lng_ref[...]
    lnb = lnb_ref[...]
    for rs in row_blocks:
        cv = jnp.concatenate([conv_buf[s, rs, :] for s in range(NSTRIP)], axis=-1)
        mu = jnp.mean(cv, axis=-1, keepdims=True)
        cen = cv - mu
        var = jnp.mean(cen * cen, axis=-1, keepdims=True)
        ln = cen * lax.rsqrt(var + EPS) * lng + lnb
        c_buf[rs, :] = _silu(ln).astype(BF16)

    pwo = jnp.dot(c_buf[...], pw_ref[...], preferred_element_type=F32)
    for rs in row_blocks:
        ycat_buf[rs, 0:D_CONV] = (pwo[rs, :] * _silu(prest_buf[G_GATE, rs, :])).astype(BF16)

    half = DK // 2
    kscale = DK ** -0.5
    for rs in row_blocks:
        cs = cos_ref[rs, :]
        sn = sin_ref[rs, :]
        for h in range(HEADS):
            lo = slice(h * DK, h * DK + half)
            hi = slice(h * DK + half, (h + 1) * DK)
            q1 = prest_buf[G_Q, rs, lo]
            q2 = prest_buf[G_Q, rs, hi]
            q_buf[rs, lo] = (q1 * cs - q2 * sn).astype(BF16)
            q_buf[rs, hi] = (q1 * sn + q2 * cs).astype(BF16)
            k1 = prest_buf[G_K, rs, lo]
            k2 = prest_buf[G_K, rs, hi]
            o1 = (k1 * cs - k2 * sn) * kscale
            o2 = (k1 * sn + k2 * cs) * kscale
            k_buf[rs, lo] = o1.astype(BF16)
            k_buf[rs, hi] = o2.astype(BF16)
            kdr = kd_ref[h, rs, :]
            kdec_buf[rs, lo] = (o1 * kdr).astype(BF16)
            kdec_buf[rs, hi] = (o2 * kdr).astype(BF16)
        v_buf[rs, :] = prest_buf[G_V, rs, :].astype(BF16)

    for h in range(HEADS):
        hs = slice(h * DK, (h + 1) * DK)
        q = q_buf[:, hs]
        v = v_buf[:, hs]
        st = state[h]
        scores = lax.dot_general(q, k_buf[:, hs], (((1,), (1,)), ((), ())),
                                 preferred_element_type=F32) * mask_ref[h]
        inner = jnp.dot(scores.astype(BF16), v, preferred_element_type=F32)
        qd = qd_ref[h]
        cross = jnp.dot(q, st.astype(BF16), preferred_element_type=F32)
        upd = lax.dot_general(kdec_buf[:, hs], v, (((0,), (0,)), ((), ())),
                              preferred_element_type=F32)
        state[h] = st * cd_ref[h] + upd
        o = inner + cross * jnp.concatenate([qd, qd], axis=-1)
        mu = jnp.mean(o, axis=-1, keepdims=True)
        cen = o - mu
        var = jnp.mean(cen * cen, axis=-1, keepdims=True)
        gn = cen * lax.rsqrt(var + EPS) * gng_ref[:, hs] + gnb_ref[:, hs]
        rg = prest_buf[G_RG, :, hs]
        ycat_buf[:, D_CONV + h * DV:D_CONV + (h + 1) * DV] = (gn * _silu(rg)).astype(BF16)

    yo = jnp.dot(ycat_buf[...], wout_ref[...], preferred_element_type=F32)
    fg = fg_ref[...]
    for rs in row_blocks:
        hres = x_ref[0, rs, :] + gate * yo[rs, :]
        ms = jnp.mean(hres * hres, axis=-1, keepdims=True)
        out_ref[0, rs, :] = hres * lax.rsqrt(ms + EPS) * fg


def _retention_tables():
    log_g = np.log(1.0 - np.exp2(-5.0 - np.arange(HEADS, dtype=np.float64)))
    idx = np.arange(TS, dtype=np.float64)
    diff = idx[:, None] - idx[None, :]
    mask = np.where(diff >= 0, np.exp(log_g[:, None, None] * np.maximum(diff, 0.0)[None]), 0.0)
    qd = np.exp(log_g[:, None] * (idx + 1.0)[None, :])
    kd = np.exp(log_g[:, None] * (TS - 1.0 - idx)[None, :])
    cd = np.exp(log_g * TS)
    qd = np.broadcast_to(qd[:, :, None], (HEADS, TS, DV // 2))
    kd = np.broadcast_to(kd[:, :, None], (HEADS, TS, DK // 2))
    cd = np.broadcast_to(cd[:, None, None], (HEADS, 1, DV))
    return tuple(jnp.asarray(np.ascontiguousarray(a), dtype=F32) for a in (mask, qd, kd, cd))


def _rope_tables(seq):
    inv_freq = 1.0 / (ROPE_BASE ** np.linspace(0.0, 1.0, DK // 2, dtype=np.float64))
    theta = np.arange(seq, dtype=np.float64)[:, None] * inv_freq[None, :]
    return jnp.asarray(np.cos(theta), dtype=F32), jnp.asarray(np.sin(theta), dtype=F32)


def _const_spec(shape):
    nd = len(shape)
    return pl.BlockSpec(shape, lambda b, t: (0,) * nd, pipeline_mode=pl.Buffered(1))


def kernel(x, c, ada_w, ada_b, norm_g, w_in, conv_w, conv_b, conv_ln_g, conv_ln_b,
           conv_pw, ret_gn_g, ret_gn_b, w_out, final_g):
    bsz, seq, _ = x.shape
    assert seq % TS == 0 and TS % RB == 0 and TS % CRB == 0

    mod = _adaln_mod(c, ada_w[0], ada_b[0]).reshape(bsz, 3, D_MODEL)
    cos, sin = _rope_tables(seq)
    mask, qd, kd, cd = _retention_tables()
    cw = conv_w[0].reshape(CONV_W, NSTRIP, LANES).transpose(1, 0, 2)
    cb = conv_b[0].reshape(NSTRIP, 1, LANES)
    row = lambda a: a.reshape(1, -1).astype(F32)

    operands = [
        (x, pl.BlockSpec((1, TS, D_MODEL), lambda b, t: (b, t, 0))),
        (mod, pl.BlockSpec((1, 3, D_MODEL), lambda b, t: (b, 0, 0))),
        (row(norm_g[0]), _const_spec((1, D_MODEL))),
        (w_in[0], pl.BlockSpec(memory_space=pl.ANY)),
        (cw, _const_spec((NSTRIP, CONV_W, LANES))),
        (cb, _const_spec((NSTRIP, 1, LANES))),
        (row(conv_ln_g[0]), _const_spec((1, D_CONV))),
        (row(conv_ln_b[0]), _const_spec((1, D_CONV))),
        (conv_pw[0], pl.BlockSpec(memory_space=pl.ANY)),
        (row(ret_gn_g[0]), _const_spec((1, D_RET))),
        (row(ret_gn_b[0]), _const_spec((1, D_RET))),
        (w_out[0], pl.BlockSpec(memory_space=pl.ANY)),
        (row(final_g), _const_spec((1, D_MODEL))),
        (cos, pl.BlockSpec((TS, DK // 2), lambda b, t: (t, 0))),
        (sin, pl.BlockSpec((TS, DK // 2), lambda b, t: (t, 0))),
        (mask, _const_spec((HEADS, TS, TS))),
        (qd, _const_spec((HEADS, TS, DV // 2))),
        (kd, _const_spec((HEADS, TS, DK // 2))),
        (cd, _const_spec((HEADS, 1, DV))),
    ]
    arrays = [a for a, _ in operands]
    in_specs = [s for _, s in operands]

    scratch = [
        pltpu.VMEM((TS, D_MODEL), BF16),
        pltpu.VMEM((NSTRIP, TS + HALO, LANES), F32),
        pltpu.VMEM((NSTRIP, SUBLANES - 1, SH_ROWS, LANES), F32),
        pltpu.VMEM((NGROUP, TS, GROUP_W), F32),
        pltpu.VMEM((NSTRIP, TS, LANES), F32),
        pltpu.VMEM((TS, D_CONV), BF16),
        pltpu.VMEM((TS, HEADS * DK), BF16),
        pltpu.VMEM((TS, HEADS * DK), BF16),
        pltpu.VMEM((TS, HEADS * DK), BF16),
        pltpu.VMEM((TS, D_RET), BF16),
        pltpu.VMEM((TS, D_CONV + D_RET), BF16),
        pltpu.VMEM((HEADS, DK, DV), F32),
        pltpu.VMEM((D_MODEL, N_IN), BF16),
        pltpu.VMEM((D_CONV, D_CONV), BF16),
        pltpu.VMEM((D_CONV + D_RET, D_MODEL), BF16),
        pltpu.VMEM((STAGE_SLOTS, STAGE_ROWS_WIDE, N_IN), F32),
        pltpu.VMEM((STAGE_SLOTS, STAGE_ROWS_SQ, D_MODEL), F32),
        pltpu.SemaphoreType.DMA((STAGE_SLOTS,)),
    ]

    return pl.pallas_call(
        _layer_kernel,
        out_shape=jax.ShapeDtypeStruct((bsz, seq, D_MODEL), x.dtype),
        grid=(bsz, seq // TS),
        in_specs=in_specs,
        out_specs=pl.BlockSpec((1, TS, D_MODEL), lambda b, t: (b, t, 0)),
        scratch_shapes=scratch,
        compiler_params=pltpu.CompilerParams(
            dimension_semantics=("arbitrary", "arbitrary"),
            vmem_limit_bytes=VMEM_LIMIT),
        name="hybrid_layer",
    )(*arrays)
```

```python
import numpy as np
import jax
import jax.numpy as jnp
from jax import lax
from jax.experimental import pallas as pl
from jax.experimental.pallas import tpu as pltpu

D_MODEL = 1024
D_CONV = 1024
D_RET = 1024
HEADS = 4
DK = 256
DV = 256
CONV_W = 31
ROPE_BASE = 10000.0
EPS = 1e-6
N_IN = 3 * D_CONV + 2 * HEADS * DK + 2 * D_RET

SUBLANES = 8
LANES = 128
MXU_N = 256
TS = 256
HALO = 32
RB = 16
CRB = 32
TAP_GATE_EVERY = 2
MOD_TN = 512
STAGE_SLOTS = 8
STAGE_ROWS_WIDE = 16
STAGE_ROWS_SQ = 64
VMEM_LIMIT = 56 * 1024 * 1024

NSLICE = D_CONV // MXU_N
GROUP_W = D_CONV
G_Q, G_K, G_V, G_GATE, G_RG = range(5)
NGROUP = 5
COL_A, COL_B = 0, D_CONV
GROUP_COL = {G_GATE: 2 * D_CONV, G_Q: 3 * D_CONV, G_K: 4 * D_CONV, G_V: 5 * D_CONV,
             G_RG: 6 * D_CONV}
NGROUP_LOOP = 4
NSTRIP = D_CONV // LANES
STRIPS_PER_SLICE = NSTRIP // NSLICE
STRIPS_PER_GROUP = NSTRIP // NGROUP_LOOP
assert GROUP_W == HEADS * DK == D_RET

TAP_OFF = [k + HALO - (CONV_W - 1) for k in range(CONV_W)]
SH_ROWS = TS + HALO - SUBLANES

NEG_LOG2E = -1.4426950408889634

F32 = jnp.float32
BF16 = jnp.bfloat16


def _sigmoid(v):
    return 1.0 / (1.0 + jnp.exp2(v * NEG_LOG2E))


def _silu(v):
    return v * _sigmoid(v)


def _mod_kernel(c_ref, w_ref, b_ref, o_ref):
    c_act = _silu(c_ref[...])
    o_ref[...] = jnp.dot(c_act, w_ref[...], preferred_element_type=F32,
                         precision=lax.Precision.HIGHEST) + b_ref[...]


def _adaln_mod(c, ada_w, ada_b):
    bsz = c.shape[0]
    n = ada_w.shape[1]
    return pl.pallas_call(
        _mod_kernel,
        out_shape=jax.ShapeDtypeStruct((bsz, n), F32),
        grid=(n // MOD_TN,),
        in_specs=[
            pl.BlockSpec((bsz, D_MODEL), lambda j: (0, 0)),
            pl.BlockSpec((D_MODEL, MOD_TN), lambda j: (0, j)),
            pl.BlockSpec((1, MOD_TN), lambda j: (0, j)),
        ],
        out_specs=pl.BlockSpec((bsz, MOD_TN), lambda j: (0, j)),
        name="adaln_mod",
    )(c, ada_w, ada_b.reshape(1, n))


def _layer_kernel(x_ref, mod_ref, ng_ref, win_hbm, cw_ref, cb_ref, lng_ref, lnb_ref,
                  pw_hbm, gng_ref, gnb_ref, wout_hbm, fg_ref, cos_ref, sin_ref,
                  mask_ref, qd_ref, kd_ref, cd_ref,
                  out_ref,
                  u_buf, a_buf, sh_buf, prest_buf, conv_buf, c_buf,
                  q_buf, kdec_buf, k_buf, v_buf, ycat_buf, state,
                  win_ref, pw_ref, wout_ref, stage_wide, stage_sq, wsem):
    t = pl.program_id(1)
    row_blocks = [slice(i * RB, (i + 1) * RB) for i in range(TS // RB)]

    def stage_weights(src_hbm, dst_ref, stage, rows):
        nchunk = src_hbm.shape[0] // rows

        def chunk_copy(i, slot):
            return pltpu.make_async_copy(
                src_hbm.at[pl.ds(pl.multiple_of(i * rows, rows), rows), :],
                stage.at[slot], wsem.at[slot])

        ahead = STAGE_SLOTS - 1
        for i in range(ahead):
            chunk_copy(i, i).start(priority=i % 2)

        def one_chunk(i, prefetch_priority):
            slot = lax.rem(i, STAGE_SLOTS)
            chunk_copy(i, slot).wait()

            @pl.when(i + ahead < nchunk)
            def _prefetch():
                chunk_copy(i + ahead, lax.rem(i + ahead, STAGE_SLOTS)).start(
                    priority=prefetch_priority)

            dst_ref[pl.ds(pl.multiple_of(i * rows, rows), rows), :] = stage[slot].astype(BF16)

        def body(j, carry):
            one_chunk(2 * j, (ahead + 0) % 2)
            one_chunk(2 * j + 1, (ahead + 1) % 2)
            return carry
        lax.fori_loop(0, nchunk // 2, body, 0)

    @pl.when((t == 0) & (pl.program_id(0) == 0))
    def _load_weights():
        stage_weights(win_hbm, win_ref, stage_wide, STAGE_ROWS_WIDE)
        stage_weights(pw_hbm, pw_ref, stage_sq, STAGE_ROWS_SQ)
        stage_weights(wout_hbm, wout_ref, stage_sq, STAGE_ROWS_SQ)

    @pl.when(t == 0)
    def _reset():
        state[...] = jnp.zeros_like(state)
        a_buf[:, 0:HALO, :] = jnp.zeros((NSTRIP, HALO, LANES), F32)

    shift = mod_ref[0, 0:1, :]
    gain = ng_ref[...] * (1.0 + mod_ref[0, 1:2, :])
    gate = mod_ref[0, 2:3, :]

    for rs in row_blocks:
        xb = x_ref[0, rs, :]
        ms = jnp.mean(xb * xb, axis=-1, keepdims=True)
        u_buf[rs, :] = (xb * lax.rsqrt(ms + EPS) * gain + shift).astype(BF16)

    def glu_and_shift(j, pab):
        pa_all, pb_all = pab
        for e in range(STRIPS_PER_SLICE):
            s = j * STRIPS_PER_SLICE + e
            pa = pa_all[:, e * LANES:(e + 1) * LANES]
            pb = pb_all[:, e * LANES:(e + 1) * LANES]
            a_buf[s, HALO:HALO + TS, :] = pa * _sigmoid(pb)
            for res in range(1, SUBLANES):
                sh_buf[s, res - 1, :, :] = a_buf[s, res:res + SH_ROWS, :]

    u_all = u_buf[...]

    def proj(c0, width):
        return jnp.dot(u_all, win_ref[:, c0:c0 + width], preferred_element_type=F32)

    def proj_ab(j):
        return proj(COL_A + j * MXU_N, MXU_N), proj(COL_B + j * MXU_N, MXU_N)

    pab_prev = proj_ab(0)
    for j in range(1, NSLICE):
        pab_next = proj_ab(j)
        glu_and_shift(j - 1, pab_prev)
        pab_prev = pab_next
    glu_and_shift(NSLICE - 1, pab_prev)

    def zero_from(v):
        bits = pltpu.bitcast(v, jnp.uint32)
        sh16 = jnp.uint32(16)
        return pltpu.bitcast(lax.shift_right_logical(lax.shift_right_logical(bits, sh16), sh16), F32)

    dyn_zero = jnp.minimum(t, 0)
    nrep = CRB // SUBLANES

    def conv_strip(s_static, token):
        s = s_static + dyn_zero
        wks = None
        for bi, r in enumerate(range(0, TS, CRB)):
            z = zero_from(token)
            cb8 = jnp.broadcast_to(cb_ref[s], (SUBLANES, LANES)) + z
            acc = jnp.concatenate([cb8] * nrep, axis=0)
            if bi % TAP_GATE_EVERY == 0:
                wks = [jnp.concatenate(
                    [jnp.broadcast_to(cw_ref[s, k:k + 1, :], (SUBLANES, LANES)) + z] * nrep, axis=0)
                    for k in range(CONV_W)]
            for res in range(SUBLANES):
                taps = [(k, off // SUBLANES) for k, off in enumerate(TAP_OFF)
                        if off % SUBLANES == res]
                span = CRB + SUBLANES * max(q for _, q in taps)
                if res == 0:
                    win = a_buf[s, r:r + span, :]
                else:
                    win = sh_buf[s, res - 1, r:r + span, :]
                for k, q in taps:
                    acc = acc + win[q * SUBLANES:q * SUBLANES + CRB, :] * wks[k]
            conv_buf[s, r:r + CRB, :] = acc
            token = acc[0:SUBLANES, :]
        return token

    token = x_ref[0, 0:SUBLANES, 0:LANES]
    for g in range(NGROUP_LOOP):
        prest_buf[g] = proj(GROUP_COL[g], GROUP_W)
        for e in range(STRIPS_PER_GROUP):
            token = conv_strip(g * STRIPS_PER_GROUP + e, token)
    prest_buf[G_RG] = proj(GROUP_COL[G_RG], GROUP_W)
    a_buf[:, 0:HALO, :] = a_buf[:, TS:TS + HALO, :]

    lng = lng_ref[...]
    lnb = lnb_ref[...]
    for rs in row_blocks:
        cv = jnp.concatenate([conv_buf[s, rs, :] for s in range(NSTRIP)], axis=-1)
        mu = jnp.mean(cv, axis=-1, keepdims=True)
        cen = cv - mu
        var = jnp.mean(cen * cen, axis=-1, keepdims=True)
        ln = cen * lax.rsqrt(var + EPS) * lng + lnb
        c_buf[rs, :] = _silu(ln).astype(BF16)

    pwo = jnp.dot(c_buf[...], pw_ref[...], preferred_element_type=F32)
    for rs in row_blocks:
        ycat_buf[rs, 0:D_CONV] = (pwo[rs, :] * _silu(prest_buf[G_GATE, rs, :])).astype(BF16)

    half = DK // 2
    kscale = DK ** -0.5
    for rs in row_blocks:
        cs = cos_ref[rs, :]
        sn = sin_ref[rs, :]
        for h in range(HEADS):
            lo = slice(h * DK, h * DK + half)
            hi = slice(h * DK + half, (h + 1) * DK)
            q1 = prest_buf[G_Q, rs, lo]
            q2 = prest_buf[G_Q, rs, hi]
            q_buf[rs, lo] = (q1 * cs - q2 * sn).astype(BF16)
            q_buf[rs, hi] = (q1 * sn + q2 * cs).astype(BF16)
            k1 = prest_buf[G_K, rs, lo]
            k2 = prest_buf[G_K, rs, hi]
            o1 = (k1 * cs - k2 * sn) * kscale
            o2 = (k1 * sn + k2 * cs) * kscale
            k_buf[rs, lo] = o1.astype(BF16)
            k_buf[rs, hi] = o2.astype(BF16)
            kdr = kd_ref[h, rs, :]
            kdec_buf[rs, lo] = (o1 * kdr).astype(BF16)
            kdec_buf[rs, hi] = (o2 * kdr).astype(BF16)
        v_buf[rs, :] = prest_buf[G_V, rs, :].astype(BF16)

    for h in range(HEADS):
        hs = slice(h * DK, (h + 1) * DK)
        q = q_buf[:, hs]
        v = v_buf[:, hs]
        st = state[h]
        scores = lax.dot_general(q, k_buf[:, hs], (((1,), (1,)), ((), ())),
                                 preferred_element_type=F32) * mask_ref[h]
        inner = jnp.dot(scores.astype(BF16), v, preferred_element_type=F32)
        qd = qd_ref[h]
        cross = jnp.dot(q, st.astype(BF16), preferred_element_type=F32)
        upd = lax.dot_general(kdec_buf[:, hs], v, (((0,), (0,)), ((), ())),
                              preferred_element_type=F32)
        state[h] = st * cd_ref[h] + upd
        o = inner + cross * jnp.concatenate([qd, qd], axis=-1)
        mu = jnp.mean(o, axis=-1, keepdims=True)
        cen = o - mu
        var = jnp.mean(cen * cen, axis=-1, keepdims=True)
        gn = cen * lax.rsqrt(var + EPS) * gng_ref[:, hs] + gnb_ref[:, hs]
        rg = prest_buf[G_RG, :, hs]
        ycat_buf[:, D_CONV + h * DV:D_CONV + (h + 1) * DV] = (gn * _silu(rg)).astype(BF16)

    yo = jnp.dot(ycat_buf[...], wout_ref[...], preferred_element_type=F32)
    fg = fg_ref[...]
    for rs in row_blocks:
        hres = x_ref[0, rs, :] + gate * yo[rs, :]
        ms = jnp.mean(hres * hres, axis=-1, keepdims=True)
        out_ref[0, rs, :] = hres * lax.rsqrt(ms + EPS) * fg


def _retention_tables():
    log_g = np.log(1.0 - np.exp2(-5.0 - np.arange(HEADS, dtype=np.float64)))
    idx = np.arange(TS, dtype=np.float64)
    diff = idx[:, None] - idx[None, :]
    mask = np.where(diff >= 0, np.exp(log_g[:, None, None] * np.maximum(diff, 0.0)[None]), 0.0)
    qd = np.exp(log_g[:, None] * (idx + 1.0)[None, :])
    kd = np.exp(log_g[:, None] * (TS - 1.0 - idx)[None, :])
    cd = np.exp(log_g * TS)
    qd = np.broadcast_to(qd[:, :, None], (HEADS, TS, DV // 2))
    kd = np.broadcast_to(kd[:, :, None], (HEADS, TS, DK // 2))
    cd = np.broadcast_to(cd[:, None, None], (HEADS, 1, DV))
    return tuple(jnp.asarray(np.ascontiguousarray(a), dtype=F32) for a in (mask, qd, kd, cd))


def _rope_tables(seq):
    inv_freq = 1.0 / (ROPE_BASE ** np.linspace(0.0, 1.0, DK // 2, dtype=np.float64))
    theta = np.arange(seq, dtype=np.float64)[:, None] * inv_freq[None, :]
    return jnp.asarray(np.cos(theta), dtype=F32), jnp.asarray(np.sin(theta), dtype=F32)


def _const_spec(shape):
    nd = len(shape)
    return pl.BlockSpec(shape, lambda b, t: (0,) * nd, pipeline_mode=pl.Buffered(1))


def kernel(x, c, ada_w, ada_b, norm_g, w_in, conv_w, conv_b, conv_ln_g, conv_ln_b,
           conv_pw, ret_gn_g, ret_gn_b, w_out, final_g):
    bsz, seq, _ = x.shape
    assert seq % TS == 0 and TS % RB == 0 and TS % CRB == 0

    mod = _adaln_mod(c, ada_w[0], ada_b[0]).reshape(bsz, 3, D_MODEL)
    cos, sin = _rope_tables(seq)
    mask, qd, kd, cd = _retention_tables()
    cw = conv_w[0].reshape(CONV_W, NSTRIP, LANES).transpose(1, 0, 2)
    cb = conv_b[0].reshape(NSTRIP, 1, LANES)
    row = lambda a: a.reshape(1, -1).astype(F32)

    operands = [
        (x, pl.BlockSpec((1, TS, D_MODEL), lambda b, t: (b, t, 0))),
        (mod, pl.BlockSpec((1, 3, D_MODEL), lambda b, t: (b, 0, 0))),
        (row(norm_g[0]), _const_spec((1, D_MODEL))),
        (w_in[0], pl.BlockSpec(memory_space=pl.ANY)),
        (cw, _const_spec((NSTRIP, CONV_W, LANES))),
        (cb, _const_spec((NSTRIP, 1, LANES))),
        (row(conv_ln_g[0]), _const_spec((1, D_CONV))),
        (row(conv_ln_b[0]), _const_spec((1, D_CONV))),
        (conv_pw[0], pl.BlockSpec(memory_space=pl.ANY)),
        (row(ret_gn_g[0]), _const_spec((1, D_RET))),
        (row(ret_gn_b[0]), _const_spec((1, D_RET))),
        (w_out[0], pl.BlockSpec(memory_space=pl.ANY)),
        (row(final_g), _const_spec((1, D_MODEL))),
        (cos, pl.BlockSpec((TS, DK // 2), lambda b, t: (t, 0))),
        (sin, pl.BlockSpec((TS, DK // 2), lambda b, t: (t, 0))),
        (mask, _const_spec((HEADS, TS, TS))),
        (qd, _const_spec((HEADS, TS, DV // 2))),
        (kd, _const_spec((HEADS, TS, DK // 2))),
        (cd, _const_spec((HEADS, 1, DV))),
    ]
    arrays = [a for a, _ in operands]
    in_specs = [s for _, s in operands]

    scratch = [
        pltpu.VMEM((TS, D_MODEL), BF16),
        pltpu.VMEM((NSTRIP, TS + HALO, LANES), F32),
        pltpu.VMEM((NSTRIP, SUBLANES - 1, SH_ROWS, LANES), F32),
        pltpu.VMEM((NGROUP, TS, GROUP_W), F32),
        pltpu.VMEM((NSTRIP, TS, LANES), F32),
        pltpu.VMEM((TS, D_CONV), BF16),
        pltpu.VMEM((TS, HEADS * DK), BF16),
        pltpu.VMEM((TS, HEADS * DK), BF16),
        pltpu.VMEM((TS, HEADS * DK), BF16),
        pltpu.VMEM((TS, D_RET), BF16),
        pltpu.VMEM((TS, D_CONV + D_RET), BF16),
        pltpu.VMEM((HEADS, DK, DV), F32),
        pltpu.VMEM((D_MODEL, N_IN), BF16),
        pltpu.VMEM((D_CONV, D_CONV), BF16),
        pltpu.VMEM((D_CONV + D_RET, D_MODEL), BF16),
        pltpu.VMEM((STAGE_SLOTS, STAGE_ROWS_WIDE, N_IN), F32),
        pltpu.VMEM((STAGE_SLOTS, STAGE_ROWS_SQ, D_MODEL), F32),
        pltpu.SemaphoreType.DMA((STAGE_SLOTS,)),
    ]

    return pl.pallas_call(
        _layer_kernel,
        out_shape=jax.ShapeDtypeStruct((bsz, seq, D_MODEL), x.dtype),
        grid=(bsz, seq // TS),
        in_specs=in_specs,
        out_specs=pl.BlockSpec((1, TS, D_MODEL), lambda b, t: (b, t, 0)),
        scratch_shapes=scratch,
        compiler_params=pltpu.CompilerParams(
            dimension_semantics=("arbitrary", "arbitrary"),
            vmem_limit_bytes=VMEM_LIMIT),
        name="hybrid_layer",
    )(*arrays)
```

```python
import numpy as np
import jax
import jax.numpy as jnp
from jax import lax
from jax.experimental import pallas as pl
from jax.experimental.pallas import tpu as pltpu

D_MODEL = 1024
D_CONV = 1024
D_RET = 1024
HEADS = 4
DK = 256
DV = 256
CONV_W = 31
ROPE_BASE = 10000.0
EPS = 1e-6
N_IN = 3 * D_CONV + 2 * HEADS * DK + 2 * D_RET

SUBLANES = 8
LANES = 128
MXU_N = 256
TS = 256
HALO = 32
RB = 16
CRB = 32
TAP_GATE_EVERY = 2
MOD_TN = 512
VMEM_LIMIT = 56 * 1024 * 1024

NSLICE = D_CONV // MXU_N
GROUP_W = D_CONV
G_Q, G_K, G_V, G_GATE, G_RG = range(5)
NGROUP = 5
COL_A, COL_B = 0, D_CONV
GROUP_COL = {G_GATE: 2 * D_CONV, G_Q: 3 * D_CONV, G_K: 4 * D_CONV, G_V: 5 * D_CONV,
             G_RG: 6 * D_CONV}
NGROUP_LOOP = 4
NSTRIP = D_CONV // LANES
STRIPS_PER_SLICE = NSTRIP // NSLICE
STRIPS_PER_GROUP = NSTRIP // NGROUP_LOOP
assert GROUP_W == HEADS * DK == D_RET

TAP_OFF = [k + HALO - (CONV_W - 1) for k in range(CONV_W)]
SH_ROWS = TS + HALO - SUBLANES

NEG_LOG2E = -1.4426950408889634

F32 = jnp.float32
BF16 = jnp.bfloat16


def _sigmoid(v):
    return 1.0 / (1.0 + jnp.exp2(v * NEG_LOG2E))


def _silu(v):
    return v * _sigmoid(v)


def _mod_kernel(c_ref, w_ref, b_ref, o_ref):
    c_act = _silu(c_ref[...])
    o_ref[...] = jnp.dot(c_act, w_ref[...], preferred_element_type=F32,
                         precision=lax.Precision.HIGHEST) + b_ref[...]


def _adaln_mod(c, ada_w, ada_b):
    bsz = c.shape[0]
    n = ada_w.shape[1]
    return pl.pallas_call(
        _mod_kernel,
        out_shape=jax.ShapeDtypeStruct((bsz, n), F32),
        grid=(n // MOD_TN,),
        in_specs=[
            pl.BlockSpec((bsz, D_MODEL), lambda j: (0, 0)),
            pl.BlockSpec((D_MODEL, MOD_TN), lambda j: (0, j)),
            pl.BlockSpec((1, MOD_TN), lambda j: (0, j)),
        ],
        out_specs=pl.BlockSpec((bsz, MOD_TN), lambda j: (0, j)),
        name="adaln_mod",
    )(c, ada_w, ada_b.reshape(1, n))


def _layer_kernel(x_ref, mod_ref, ng_ref, win_hbm, cw_ref, cb_ref, lng_ref, lnb_ref,
                  pw_hbm, gng_ref, gnb_ref, wout_hbm, fg_ref, cos_ref, sin_ref,
                  mask_ref, qd_ref, kd_ref, cd_ref,
                  out_ref,
                  u_buf, a_buf, sh_buf, prest_buf, conv_buf, c_buf,
                  q_buf, kdec_buf, k_buf, v_buf, ycat_buf, state,
                  win_ref, pw_ref, wout_ref, wsem):
    t = pl.program_id(1)
    row_blocks = [slice(i * RB, (i + 1) * RB) for i in range(TS // RB)]

    @pl.when((t == 0) & (pl.program_id(0) == 0))
    def _load_weights():
        blocks = [(src, dst, r0, c0)
                  for src, dst in ((win_hbm, win_ref), (pw_hbm, pw_ref), (wout_hbm, wout_ref))
                  for r0 in range(0, src.shape[0], TS)
                  for c0 in range(0, src.shape[1], GROUP_W)]
        ahead = NGROUP - 1

        def block_copy(i):
            src, _, r0, c0 = blocks[i]
            slot = i % NGROUP
            return pltpu.make_async_copy(src.at[r0:r0 + TS, c0:c0 + GROUP_W],
                                         prest_buf.at[slot], wsem.at[slot])

        for i in range(ahead):
            block_copy(i).start()
        for i, (_, dst, r0, c0) in enumerate(blocks):
            block_copy(i).wait()
            if i + ahead < len(blocks):
                block_copy(i + ahead).start()
            dst[r0:r0 + TS, c0:c0 + GROUP_W] = prest_buf[i % NGROUP].astype(BF16)

    @pl.when(t == 0)
    def _reset():
        state[...] = jnp.zeros_like(state)
        a_buf[:, 0:HALO, :] = jnp.zeros((NSTRIP, HALO, LANES), F32)

    shift = mod_ref[0, 0:1, :]
    gain = ng_ref[...] * (1.0 + mod_ref[0, 1:2, :])
    gate = mod_ref[0, 2:3, :]

    for rs in row_blocks:
        xb = x_ref[0, rs, :]
        ms = jnp.mean(xb * xb, axis=-1, keepdims=True)
        u_buf[rs, :] = (xb * lax.rsqrt(ms + EPS) * gain + shift).astype(BF16)

    def glu_and_shift(j, pab):
        pa_all, pb_all = pab
        for e in range(STRIPS_PER_SLICE):
            s = j * STRIPS_PER_SLICE + e
            pa = pa_all[:, e * LANES:(e + 1) * LANES]
            pb = pb_all[:, e * LANES:(e + 1) * LANES]
            a_buf[s, HALO:HALO + TS, :] = pa * _sigmoid(pb)
            for res in range(1, SUBLANES):
                sh_buf[s, res - 1, :, :] = a_buf[s, res:res + SH_ROWS, :]

    u_all = u_buf[...]

    def proj(c0, width):
        return jnp.dot(u_all, win_ref[:, c0:c0 + width], preferred_element_type=F32)

    def proj_ab(j):
        return proj(COL_A + j * MXU_N, MXU_N), proj(COL_B + j * MXU_N, MXU_N)

    pab_prev = proj_ab(0)
    for j in range(1, NSLICE):
        pab_next = proj_ab(j)
        glu_and_shift(j - 1, pab_prev)
        pab_prev = pab_next
    glu_and_shift(NSLICE - 1, pab_prev)

    def zero_from(v):
        bits = pltpu.bitcast(v, jnp.uint32)
        sh16 = jnp.uint32(16)
        return pltpu.bitcast(lax.shift_right_logical(lax.shift_right_logical(bits, sh16), sh16), F32)

    dyn_zero = jnp.minimum(t, 0)
    nrep = CRB // SUBLANES

    def conv_strip(s_static, token):
        s = s_static + dyn_zero
        wks = None
        for bi, r in enumerate(range(0, TS, CRB)):
            z = zero_from(token)
            cb8 = jnp.broadcast_to(cb_ref[s], (SUBLANES, LANES)) + z
            acc = jnp.concatenate([cb8] * nrep, axis=0)
            if bi % TAP_GATE_EVERY == 0:
                wks = [jnp.concatenate(
                    [jnp.broadcast_to(cw_ref[s, k:k + 1, :], (SUBLANES, LANES)) + z] * nrep, axis=0)
                    for k in range(CONV_W)]
            for res in range(SUBLANES):
                taps = [(k, off // SUBLANES) for k, off in enumerate(TAP_OFF)
                        if off % SUBLANES == res]
                span = CRB + SUBLANES * max(q for _, q in taps)
                if res == 0:
                    win = a_buf[s, r:r + span, :]
                else:
                    win = sh_buf[s, res - 1, r:r + span, :]
                for k, q in taps:
                    acc = acc + win[q * SUBLANES:q * SUBLANES + CRB, :] * wks[k]
            conv_buf[s, r:r + CRB, :] = acc
            token = acc[0:SUBLANES, :]
        return token

    token = x_ref[0, 0:SUBLANES, 0:LANES]
    for g in range(NGROUP_LOOP):
        prest_buf[g] = proj(GROUP_COL[g], GROUP_W)
        for e in range(STRIPS_PER_GROUP):
            token = conv_strip(g * STRIPS_PER_GROUP + e, token)
    prest_buf[G_RG] = proj(GROUP_COL[G_RG], GROUP_W)
    a_buf[:, 0:HALO, :] = a_buf[:, TS:TS + HALO, :]

    lng = lng_ref[...]
    lnb = lnb_ref[...]
    for rs in row_blocks:
        cv = jnp.concatenate([conv_buf[s, rs, :] for s in range(NSTRIP)], axis=-1)
        mu = jnp.mean(cv, axis=-1, keepdims=True)
        cen = cv - mu
        var = jnp.mean(cen * cen, axis=-1, keepdims=True)
        ln = cen * lax.rsqrt(var + EPS) * lng + lnb
        c_buf[rs, :] = _silu(ln).astype(BF16)

    pwo = jnp.dot(c_buf[...], pw_ref[...], preferred_element_type=F32)
    for rs in row_blocks:
        ycat_buf[rs, 0:D_CONV] = (pwo[rs, :] * _silu(prest_buf[G_GATE, rs, :])).astype(BF16)

    half = DK // 2
    kscale = DK ** -0.5
    for rs in row_blocks:
        cs = cos_ref[rs, :]
        sn = sin_ref[rs, :]
        for h in range(HEADS):
            lo = slice(h * DK, h * DK + half)
            hi = slice(h * DK + half, (h + 1) * DK)
            q1 = prest_buf[G_Q, rs, lo]
            q2 = prest_buf[G_Q, rs, hi]
            q_buf[rs, lo] = (q1 * cs - q2 * sn).astype(BF16)
            q_buf[rs, hi] = (q1 * sn + q2 * cs).astype(BF16)
            k1 = prest_buf[G_K, rs, lo]
            k2 = prest_buf[G_K, rs, hi]
            o1 = (k1 * cs - k2 * sn) * kscale
            o2 = (k1 * sn + k2 * cs) * kscale
            k_buf[rs, lo] = o1.astype(BF16)
            k_buf[rs, hi] = o2.astype(BF16)
            kdr = kd_ref[h, rs, :]
            kdec_buf[rs, lo] = (o1 * kdr).astype(BF16)
            kdec_buf[rs, hi] = (o2 * kdr).astype(BF16)
        v_buf[rs, :] = prest_buf[G_V, rs, :].astype(BF16)

    for h in range(HEADS):
        hs = slice(h * DK, (h + 1) * DK)
        q = q_buf[:, hs]
        v = v_buf[:, hs]
        st = state[h]
        scores = lax.dot_general(q, k_buf[:, hs], (((1,), (1,)), ((), ())),
                                 preferred_element_type=F32) * mask_ref[h]
        inner = jnp.dot(scores.astype(BF16), v, preferred_element_type=F32)
        qd = qd_ref[h]
        cross = jnp.dot(q, st.astype(BF16), preferred_element_type=F32)
        upd = lax.dot_general(kdec_buf[:, hs], v, (((0,), (0,)), ((), ())),
                              preferred_element_type=F32)
        state[h] = st * cd_ref[h] + upd
        o = inner + cross * jnp.concatenate([qd, qd], axis=-1)
        mu = jnp.mean(o, axis=-1, keepdims=True)
        cen = o - mu
        var = jnp.mean(cen * cen, axis=-1, keepdims=True)
        gn = cen * lax.rsqrt(var + EPS) * gng_ref[:, hs] + gnb_ref[:, hs]
        rg = prest_buf[G_RG, :, hs]
        ycat_buf[:, D_CONV + h * DV:D_CONV + (h + 1) * DV] = (gn * _silu(rg)).astype(BF16)

    yo = jnp.dot(ycat_buf[...], wout_ref[...], preferred_element_type=F32)
    fg = fg_ref[...]
    for rs in row_blocks:
        hres = x_ref[0, rs, :] + gate * yo[rs, :]
        ms = jnp.mean(hres * hres, axis=-1, keepdims=True)
        out_ref[0, rs, :] = hres * lax.rsqrt(ms + EPS) * fg


def _retention_tables():
    log_g = np.log(1.0 - np.exp2(-5.0 - np.arange(HEADS, dtype=np.float64)))
    idx = np.arange(TS, dtype=np.float64)
    diff = idx[:, None] - idx[None, :]
    mask = np.where(diff >= 0, np.exp(log_g[:, None, None] * np.maximum(diff, 0.0)[None]), 0.0)
    qd = np.exp(log_g[:, None] * (idx + 1.0)[None, :])
    kd = np.exp(log_g[:, None] * (TS - 1.0 - idx)[None, :])
    cd = np.exp(log_g * TS)
    qd = np.broadcast_to(qd[:, :, None], (HEADS, TS, DV // 2))
    kd = np.broadcast_to(kd[:, :, None], (HEADS, TS, DK // 2))
    cd = np.broadcast_to(cd[:, None, None], (HEADS, 1, DV))
    return tuple(jnp.asarray(np.ascontiguousarray(a), dtype=F32) for a in (mask, qd, kd, cd))


def _rope_tables(seq):
    inv_freq = 1.0 / (ROPE_BASE ** np.linspace(0.0, 1.0, DK // 2, dtype=np.float64))
    theta = np.arange(seq, dtype=np.float64)[:, None] * inv_freq[None, :]
    return jnp.asarray(np.cos(theta), dtype=F32), jnp.asarray(np.sin(theta), dtype=F32)


def _const_spec(shape):
    nd = len(shape)
    return pl.BlockSpec(shape, lambda b, t: (0,) * nd, pipeline_mode=pl.Buffered(1))


def kernel(x, c, ada_w, ada_b, norm_g, w_in, conv_w, conv_b, conv_ln_g, conv_ln_b,
           conv_pw, ret_gn_g, ret_gn_b, w_out, final_g):
    bsz, seq, _ = x.shape
    assert seq % TS == 0 and TS % RB == 0 and TS % CRB == 0

    mod = _adaln_mod(c, ada_w[0], ada_b[0]).reshape(bsz, 3, D_MODEL)
    cos, sin = _rope_tables(seq)
    mask, qd, kd, cd = _retention_tables()
    cw = conv_w[0].reshape(CONV_W, NSTRIP, LANES).transpose(1, 0, 2)
    cb = conv_b[0].reshape(NSTRIP, 1, LANES)
    row = lambda a: a.reshape(1, -1).astype(F32)

    operands = [
        (x, pl.BlockSpec((1, TS, D_MODEL), lambda b, t: (b, t, 0))),
        (mod, pl.BlockSpec((1, 3, D_MODEL), lambda b, t: (b, 0, 0))),
        (row(norm_g[0]), _const_spec((1, D_MODEL))),
        (w_in[0], pl.BlockSpec(memory_space=pl.ANY)),
        (cw, _const_spec((NSTRIP, CONV_W, LANES))),
        (cb, _const_spec((NSTRIP, 1, LANES))),
        (row(conv_ln_g[0]), _const_spec((1, D_CONV))),
        (row(conv_ln_b[0]), _const_spec((1, D_CONV))),
        (conv_pw[0], pl.BlockSpec(memory_space=pl.ANY)),
        (row(ret_gn_g[0]), _const_spec((1, D_RET))),
        (row(ret_gn_b[0]), _const_spec((1, D_RET))),
        (w_out[0], pl.BlockSpec(memory_space=pl.ANY)),
        (row(final_g), _const_spec((1, D_MODEL))),
        (cos, pl.BlockSpec((TS, DK // 2), lambda b, t: (t, 0))),
        (sin, pl.BlockSpec((TS, DK // 2), lambda b, t: (t, 0))),
        (mask, _const_spec((HEADS, TS, TS))),
        (qd, _const_spec((HEADS, TS, DV // 2))),
        (kd, _const_spec((HEADS, TS, DK // 2))),
        (cd, _const_spec((HEADS, 1, DV))),
    ]
    arrays = [a for a, _ in operands]
    in_specs = [s for _, s in operands]

    scratch = [
        pltpu.VMEM((TS, D_MODEL), BF16),
        pltpu.VMEM((NSTRIP, TS + HALO, LANES), F32),
        pltpu.VMEM((NSTRIP, SUBLANES - 1, SH_ROWS, LANES), F32),
        pltpu.VMEM((NGROUP, TS, GROUP_W), F32),
        pltpu.VMEM((NSTRIP, TS, LANES), F32),
        pltpu.VMEM((TS, D_CONV), BF16),
        pltpu.VMEM((TS, HEADS * DK), BF16),
        pltpu.VMEM((TS, HEADS * DK), BF16),
        pltpu.VMEM((TS, HEADS * DK), BF16),
        pltpu.VMEM((TS, D_RET), BF16),
        pltpu.VMEM((TS, D_CONV + D_RET), BF16),
        pltpu.VMEM((HEADS, DK, DV), F32),
        pltpu.VMEM((D_MODEL, N_IN), BF16),
        pltpu.VMEM((D_CONV, D_CONV), BF16),
        pltpu.VMEM((D_CONV + D_RET, D_MODEL), BF16),
        pltpu.SemaphoreType.DMA((NGROUP,)),
    ]

    return pl.pallas_call(
        _layer_kernel,
        out_shape=jax.ShapeDtypeStruct((bsz, seq, D_MODEL), x.dtype),
        grid=(bsz, seq // TS),
        in_specs=in_specs,
        out_specs=pl.BlockSpec((1, TS, D_MODEL), lambda b, t: (b, t, 0)),
        scratch_shapes=scratch,
        compiler_params=pltpu.CompilerParams(
            dimension_semantics=("arbitrary", "arbitrary"),
            vmem_limit_bytes=VMEM_LIMIT),
        name="hybrid_layer",
    )(*arrays)
```

```python
import numpy as np
import jax
import jax.numpy as jnp
from jax import lax
from jax.experimental import pallas as pl
from jax.experimental.pallas import tpu as pltpu

D_MODEL = 1024
D_CONV = 1024
D_RET = 1024
HEADS = 4
DK = 256
DV = 256
CONV_W = 31
ROPE_BASE = 10000.0
EPS = 1e-6
N_IN = 3 * D_CONV + 2 * HEADS * DK + 2 * D_RET

SUBLANES = 8
LANES = 128
MXU_N = 256
TS = 256
HALO = 32
RB = 16
CRB = 32
TAP_GATE_EVERY = 2
MOD_TN = 1024
VMEM_LIMIT = 56 * 1024 * 1024

NSLICE = D_CONV // MXU_N
GROUP_W = D_CONV
G_Q, G_K, G_V, G_GATE, G_RG = range(5)
NGROUP = 5
COL_A, COL_B = 0, D_CONV
GROUP_COL = {G_GATE: 2 * D_CONV, G_Q: 3 * D_CONV, G_K: 4 * D_CONV, G_V: 5 * D_CONV,
             G_RG: 6 * D_CONV}
NGROUP_LOOP = 4
NSTRIP = D_CONV // LANES
STRIPS_PER_SLICE = NSTRIP // NSLICE
STRIPS_PER_GROUP = NSTRIP // NGROUP_LOOP
assert GROUP_W == HEADS * DK == D_RET

TAP_OFF = [k + HALO - (CONV_W - 1) for k in range(CONV_W)]
SH_ROWS = TS + HALO - SUBLANES

NEG_LOG2E = -1.4426950408889634

F32 = jnp.float32
BF16 = jnp.bfloat16


def _sigmoid(v):
    return 1.0 / (1.0 + jnp.exp2(v * NEG_LOG2E))


def _silu(v):
    return v * _sigmoid(v)


def _mod_kernel(c_ref, w_ref, b_ref, o_ref):
    c_act = _silu(c_ref[...])
    o_ref[...] = jnp.dot(c_act, w_ref[...], preferred_element_type=F32) + b_ref[...]


def _adaln_mod(c, ada_w, ada_b):
    bsz = c.shape[0]
    n = ada_w.shape[1]
    return pl.pallas_call(
        _mod_kernel,
        out_shape=jax.ShapeDtypeStruct((bsz, n), F32),
        grid=(n // MOD_TN,),
        in_specs=[
            pl.BlockSpec((bsz, D_MODEL), lambda j: (0, 0)),
            pl.BlockSpec((D_MODEL, MOD_TN), lambda j: (0, j)),
            pl.BlockSpec((1, MOD_TN), lambda j: (0, j)),
        ],
        out_specs=pl.BlockSpec((bsz, MOD_TN), lambda j: (0, j)),
        name="adaln_mod",
    )(c, ada_w, ada_b.reshape(1, n))


def _layer_kernel(x_ref, mod_ref, ng_ref, win_hbm, cw_ref, cb_ref, lng_ref, lnb_ref,
                  pw_hbm, gng_ref, gnb_ref, wout_hbm, fg_ref, cos_ref, sin_ref,
                  mask_ref, qd_ref, kd_ref, cd_ref,
                  out_ref,
                  u_buf, a_buf, sh_buf, prest_buf, conv_buf, c_buf,
                  q_buf, kdec_buf, k_buf, v_buf, ycat_buf, state,
                  win_ref, pw_ref, wout_ref, wsem):
    t = pl.program_id(1)
    row_blocks = [slice(i * RB, (i + 1) * RB) for i in range(TS // RB)]

    @pl.when((t == 0) & (pl.program_id(0) == 0))
    def _load_weights():
        blocks = [(src, dst, r0, c0)
                  for src, dst in ((win_hbm, win_ref), (pw_hbm, pw_ref), (wout_hbm, wout_ref))
                  for r0 in range(0, src.shape[0], TS)
                  for c0 in range(0, src.shape[1], GROUP_W)]
        ahead = NGROUP - 1

        def block_copy(i):
            src, _, r0, c0 = blocks[i]
            slot = i % NGROUP
            return pltpu.make_async_copy(src.at[r0:r0 + TS, c0:c0 + GROUP_W],
                                         prest_buf.at[slot], wsem.at[slot])

        for i in range(ahead):
            block_copy(i).start()
        for i, (_, dst, r0, c0) in enumerate(blocks):
            block_copy(i).wait()
            if i + ahead < len(blocks):
                block_copy(i + ahead).start()
            dst[r0:r0 + TS, c0:c0 + GROUP_W] = prest_buf[i % NGROUP].astype(BF16)

    @pl.when(t == 0)
    def _reset():
        state[...] = jnp.zeros_like(state)
        a_buf[:, 0:HALO, :] = jnp.zeros((NSTRIP, HALO, LANES), F32)

    shift = mod_ref[0, 0:1, :]
    gain = ng_ref[...] * (1.0 + mod_ref[0, 1:2, :])
    gate = mod_ref[0, 2:3, :]

    for rs in row_blocks:
        xb = x_ref[0, rs, :]
        ms = jnp.mean(xb * xb, axis=-1, keepdims=True)
        u_buf[rs, :] = (xb * lax.rsqrt(ms + EPS) * gain + shift).astype(BF16)

    def glu_and_shift(j, pab):
        pa_all, pb_all = pab
        for e in range(STRIPS_PER_SLICE):
            s = j * STRIPS_PER_SLICE + e
            pa = pa_all[:, e * LANES:(e + 1) * LANES]
            pb = pb_all[:, e * LANES:(e + 1) * LANES]
            a_buf[s, HALO:HALO + TS, :] = pa * _sigmoid(pb)
            for res in range(1, SUBLANES):
                sh_buf[s, res - 1, :, :] = a_buf[s, res:res + SH_ROWS, :]

    u_all = u_buf[...]

    def proj(c0, width):
        return jnp.dot(u_all, win_ref[:, c0:c0 + width], preferred_element_type=F32)

    def proj_ab(j):
        return proj(COL_A + j * MXU_N, MXU_N), proj(COL_B + j * MXU_N, MXU_N)

    pab_prev = proj_ab(0)
    for j in range(1, NSLICE):
        pab_next = proj_ab(j)
        glu_and_shift(j - 1, pab_prev)
        pab_prev = pab_next
    glu_and_shift(NSLICE - 1, pab_prev)

    def zero_from(v):
        bits = pltpu.bitcast(v, jnp.uint32)
        sh16 = jnp.uint32(16)
        return pltpu.bitcast(lax.shift_right_logical(lax.shift_right_logical(bits, sh16), sh16), F32)

    dyn_zero = jnp.minimum(t, 0)
    nrep = CRB // SUBLANES

    def conv_strip(s_static, token):
        s = s_static + dyn_zero
        wks = None
        for bi, r in enumerate(range(0, TS, CRB)):
            z = zero_from(token)
            cb8 = jnp.broadcast_to(cb_ref[s], (SUBLANES, LANES)) + z
            acc = jnp.concatenate([cb8] * nrep, axis=0)
            if bi % TAP_GATE_EVERY == 0:
                wks = [jnp.concatenate(
                    [jnp.broadcast_to(cw_ref[s, k:k + 1, :], (SUBLANES, LANES)) + z] * nrep, axis=0)
                    for k in range(CONV_W)]
            for res in range(SUBLANES):
                taps = [(k, off // SUBLANES) for k, off in enumerate(TAP_OFF)
                        if off % SUBLANES == res]
                span = CRB + SUBLANES * max(q for _, q in taps)
                if res == 0:
                    win = a_buf[s, r:r + span, :]
                else:
                    win = sh_buf[s, res - 1, r:r + span, :]
                for k, q in taps:
                    acc = acc + win[q * SUBLANES:q * SUBLANES + CRB, :] * wks[k]
            conv_buf[s, r:r + CRB, :] = acc
            token = acc[0:SUBLANES, :]
        return token

    token = x_ref[0, 0:SUBLANES, 0:LANES]
    for g in range(NGROUP_LOOP):
        prest_buf[g] = proj(GROUP_COL[g], GROUP_W)
        for e in range(STRIPS_PER_GROUP):
            token = conv_strip(g * STRIPS_PER_GROUP + e, token)
    prest_buf[G_RG] = proj(GROUP_COL[G_RG], GROUP_W)
    a_buf[:, 0:HALO, :] = a_buf[:, TS:TS + HALO, :]

    lng = lng_ref[...]
    lnb = lnb_ref[...]
    for rs in row_blocks:
        cv = jnp.concatenate([conv_buf[s, rs, :] for s in range(NSTRIP)], axis=-1)
        mu = jnp.mean(cv, axis=-1, keepdims=True)
        cen = cv - mu
        var = jnp.mean(cen * cen, axis=-1, keepdims=True)
        ln = cen * lax.rsqrt(var + EPS) * lng + lnb
        c_buf[rs, :] = _silu(ln).astype(BF16)

    pwo = jnp.dot(c_buf[...], pw_ref[...], preferred_element_type=F32)
    for rs in row_blocks:
        ycat_buf[rs, 0:D_CONV] = (pwo[rs, :] * _silu(prest_buf[G_GATE, rs, :])).astype(BF16)

    half = DK // 2
    kscale = DK ** -0.5
    for rs in row_blocks:
        cs = cos_ref[rs, :]
        sn = sin_ref[rs, :]
        for h in range(HEADS):
            lo = slice(h * DK, h * DK + half)
            hi = slice(h * DK + half, (h + 1) * DK)
            q1 = prest_buf[G_Q, rs, lo]
            q2 = prest_buf[G_Q, rs, hi]
            q_buf[rs, lo] = (q1 * cs - q2 * sn).astype(BF16)
            q_buf[rs, hi] = (q1 * sn + q2 * cs).astype(BF16)
            k1 = prest_buf[G_K, rs, lo]
            k2 = prest_buf[G_K, rs, hi]
            o1 = (k1 * cs - k2 * sn) * kscale
            o2 = (k1 * sn + k2 * cs) * kscale
            k_buf[rs, lo] = o1.astype(BF16)
            k_buf[rs, hi] = o2.astype(BF16)
            kdr = kd_ref[h, rs, :]
            kdec_buf[rs, lo] = (o1 * kdr).astype(BF16)
            kdec_buf[rs, hi] = (o2 * kdr).astype(BF16)
        v_buf[rs, :] = prest_buf[G_V, rs, :].astype(BF16)

    for h in range(HEADS):
        hs = slice(h * DK, (h + 1) * DK)
        q = q_buf[:, hs]
        v = v_buf[:, hs]
        st = state[h]
        scores = lax.dot_general(q, k_buf[:, hs], (((1,), (1,)), ((), ())),
                                 preferred_element_type=F32) * mask_ref[h]
        inner = jnp.dot(scores.astype(BF16), v, preferred_element_type=F32)
        qd = qd_ref[h]
        cross = jnp.dot(q, st.astype(BF16), preferred_element_type=F32)
        upd = lax.dot_general(kdec_buf[:, hs], v, (((0,), (0,)), ((), ())),
                              preferred_element_type=F32)
        state[h] = st * cd_ref[h] + upd
        o = inner + cross * jnp.concatenate([qd, qd], axis=-1)
        mu = jnp.mean(o, axis=-1, keepdims=True)
        cen = o - mu
        var = jnp.mean(cen * cen, axis=-1, keepdims=True)
        gn = cen * lax.rsqrt(var + EPS) * gng_ref[:, hs] + gnb_ref[:, hs]
        rg = prest_buf[G_RG, :, hs]
        ycat_buf[:, D_CONV + h * DV:D_CONV + (h + 1) * DV] = (gn * _silu(rg)).astype(BF16)

    yo = jnp.dot(ycat_buf[...], wout_ref[...], preferred_element_type=F32)
    fg = fg_ref[...]
    for rs in row_blocks:
        hres = x_ref[0, rs, :] + gate * yo[rs, :]
        ms = jnp.mean(hres * hres, axis=-1, keepdims=True)
        out_ref[0, rs, :] = hres * lax.rsqrt(ms + EPS) * fg


def _retention_tables():
    log_g = np.log(1.0 - np.exp2(-5.0 - np.arange(HEADS, dtype=np.float64)))
    idx = np.arange(TS, dtype=np.float64)
    diff = idx[:, None] - idx[None, :]
    mask = np.where(diff >= 0, np.exp(log_g[:, None, None] * np.maximum(diff, 0.0)[None]), 0.0)
    qd = np.exp(log_g[:, None] * (idx + 1.0)[None, :])
    kd = np.exp(log_g[:, None] * (TS - 1.0 - idx)[None, :])
    cd = np.exp(log_g * TS)
    qd = np.broadcast_to(qd[:, :, None], (HEADS, TS, DV // 2))
    kd = np.broadcast_to(kd[:, :, None], (HEADS, TS, DK // 2))
    cd = np.broadcast_to(cd[:, None, None], (HEADS, 1, DV))
    return tuple(jnp.asarray(np.ascontiguousarray(a), dtype=F32) for a in (mask, qd, kd, cd))


def _rope_tables(seq):
    inv_freq = 1.0 / (ROPE_BASE ** np.linspace(0.0, 1.0, DK // 2, dtype=np.float64))
    theta = np.arange(seq, dtype=np.float64)[:, None] * inv_freq[None, :]
    return jnp.asarray(np.cos(theta), dtype=F32), jnp.asarray(np.sin(theta), dtype=F32)


def _const_spec(shape):
    nd = len(shape)
    return pl.BlockSpec(shape, lambda b, t: (0,) * nd, pipeline_mode=pl.Buffered(1))


def kernel(x, c, ada_w, ada_b, norm_g, w_in, conv_w, conv_b, conv_ln_g, conv_ln_b,
           conv_pw, ret_gn_g, ret_gn_b, w_out, final_g):
    bsz, seq, _ = x.shape
    assert seq % TS == 0 and TS % RB == 0 and TS % CRB == 0

    mod = _adaln_mod(c, ada_w[0], ada_b[0]).reshape(bsz, 3, D_MODEL)
    cos, sin = _rope_tables(seq)
    mask, qd, kd, cd = _retention_tables()
    cw = conv_w[0].reshape(CONV_W, NSTRIP, LANES).transpose(1, 0, 2)
    cb = conv_b[0].reshape(NSTRIP, 1, LANES)
    row = lambda a: a.reshape(1, -1).astype(F32)

    operands = [
        (x, pl.BlockSpec((1, TS, D_MODEL), lambda b, t: (b, t, 0))),
        (mod, pl.BlockSpec((1, 3, D_MODEL), lambda b, t: (b, 0, 0))),
        (row(norm_g[0]), _const_spec((1, D_MODEL))),
        (w_in[0], pl.BlockSpec(memory_space=pl.ANY)),
        (cw, _const_spec((NSTRIP, CONV_W, LANES))),
        (cb, _const_spec((NSTRIP, 1, LANES))),
        (row(conv_ln_g[0]), _const_spec((1, D_CONV))),
        (row(conv_ln_b[0]), _const_spec((1, D_CONV))),
        (conv_pw[0], pl.BlockSpec(memory_space=pl.ANY)),
        (row(ret_gn_g[0]), _const_spec((1, D_RET))),
        (row(ret_gn_b[0]), _const_spec((1, D_RET))),
        (w_out[0], pl.BlockSpec(memory_space=pl.ANY)),
        (row(final_g), _const_spec((1, D_MODEL))),
        (cos, pl.BlockSpec((TS, DK // 2), lambda b, t: (t, 0))),
        (sin, pl.BlockSpec((TS, DK // 2), lambda b, t: (t, 0))),
        (mask, _const_spec((HEADS, TS, TS))),
        (qd, _const_spec((HEADS, TS, DV // 2))),
        (kd, _const_spec((HEADS, TS, DK // 2))),
        (cd, _const_spec((HEADS, 1, DV))),
    ]
    arrays = [a for a, _ in operands]
    in_specs = [s for _, s in operands]

    scratch = [
        pltpu.VMEM((TS, D_MODEL), BF16),
        pltpu.VMEM((NSTRIP, TS + HALO, LANES), F32),
        pltpu.VMEM((NSTRIP, SUBLANES - 1, SH_ROWS, LANES), F32),
        pltpu.VMEM((NGROUP, TS, GROUP_W), F32),
        pltpu.VMEM((NSTRIP, TS, LANES), F32),
        pltpu.VMEM((TS, D_CONV), BF16),
        pltpu.VMEM((TS, HEADS * DK), BF16),
        pltpu.VMEM((TS, HEADS * DK), BF16),
        pltpu.VMEM((TS, HEADS * DK), BF16),
        pltpu.VMEM((TS, D_RET), BF16),
        pltpu.VMEM((TS, D_CONV + D_RET), BF16),
        pltpu.VMEM((HEADS, DK, DV), F32),
        pltpu.VMEM((D_MODEL, N_IN), BF16),
        pltpu.VMEM((D_CONV, D_CONV), BF16),
        pltpu.VMEM((D_CONV + D_RET, D_MODEL), BF16),
        pltpu.SemaphoreType.DMA((NGROUP,)),
    ]

    return pl.pallas_call(
        _layer_kernel,
        out_shape=jax.ShapeDtypeStruct((bsz, seq, D_MODEL), x.dtype),
        grid=(bsz, seq // TS),
        in_specs=in_specs,
        out_specs=pl.BlockSpec((1, TS, D_MODEL), lambda b, t: (b, t, 0)),
        scratch_shapes=scratch,
        compiler_params=pltpu.CompilerParams(
            dimension_semantics=("arbitrary", "arbitrary"),
            vmem_limit_bytes=VMEM_LIMIT),
        name="hybrid_layer",
    )(*arrays)
```

```python
import numpy as np
import jax
import jax.numpy as jnp
from jax import lax
from jax.experimental import pallas as pl
from jax.experimental.pallas import tpu as pltpu

D_MODEL = 1024
D_CONV = 1024
D_RET = 1024
HEADS = 4
DK = 256
DV = 256
CONV_W = 31
ROPE_BASE = 10000.0
EPS = 1e-6
N_IN = 3 * D_CONV + 2 * HEADS * DK + 2 * D_RET

SUBLANES = 8
LANES = 128
MXU_N = 256
TS = 256
HALO = 32
RB = 16
CRB = 32
TAP_GATE_EVERY = 2
MOD_TN = 1024
STAGE_PLANES = 5
VMEM_LIMIT = 56 * 1024 * 1024

NSLICE = D_CONV // MXU_N
GROUP_W = D_CONV
G_Q, G_K, G_V, G_GATE, G_RG = range(5)
NGROUP = 5
COL_A, COL_B = 0, D_CONV
GROUP_COL = {G_GATE: 2 * D_CONV, G_Q: 3 * D_CONV, G_K: 4 * D_CONV, G_V: 5 * D_CONV,
             G_RG: 6 * D_CONV}
NGROUP_LOOP = 4
NSTRIP = D_CONV // LANES
STRIPS_PER_SLICE = NSTRIP // NSLICE
STRIPS_PER_GROUP = NSTRIP // NGROUP_LOOP
assert GROUP_W == HEADS * DK == D_RET

TAP_OFF = [k + HALO - (CONV_W - 1) for k in range(CONV_W)]
SH_ROWS = TS + HALO - SUBLANES

NEG_LOG2E = -1.4426950408889634

F32 = jnp.float32
BF16 = jnp.bfloat16


def _sigmoid(v):
    return 1.0 / (1.0 + jnp.exp2(v * NEG_LOG2E))


def _silu(v):
    return v * _sigmoid(v)


def _mod_kernel(c_ref, w_ref, b_ref, o_ref):
    c_act = _silu(c_ref[...])
    o_ref[...] = jnp.dot(c_act, w_ref[...], preferred_element_type=F32) + b_ref[...]


def _adaln_mod(c, ada_w, ada_b):
    bsz = c.shape[0]
    n = ada_w.shape[1]
    return pl.pallas_call(
        _mod_kernel,
        out_shape=jax.ShapeDtypeStruct((bsz, n), F32),
        grid=(n // MOD_TN,),
        in_specs=[
            pl.BlockSpec((bsz, D_MODEL), lambda j: (0, 0)),
            pl.BlockSpec((D_MODEL, MOD_TN), lambda j: (0, j)),
            pl.BlockSpec((1, MOD_TN), lambda j: (0, j)),
        ],
        out_specs=pl.BlockSpec((bsz, MOD_TN), lambda j: (0, j)),
        name="adaln_mod",
    )(c, ada_w, ada_b.reshape(1, n))


def _layer_kernel(x_ref, mod_ref, ng_ref, win_hbm, cw_ref, cb_ref, lng_ref, lnb_ref,
                  pw_hbm, gng_ref, gnb_ref, wout_hbm, fg_ref, cos_ref, sin_ref,
                  mask_ref, qd_ref, kd_ref, cd_ref,
                  out_ref,
                  u_buf, a_buf, sh_buf, prest_buf, conv_buf, c_buf,
                  q_buf, kdec_buf, k_buf, v_buf, ycat_buf, state,
                  win_ref, pw_ref, wout_ref, stage_buf, wsem):
    t = pl.program_id(1)
    row_blocks = [slice(i * RB, (i + 1) * RB) for i in range(TS // RB)]

    @pl.when((t == 0) & (pl.program_id(0) == 0))
    def _load_weights():
        blocks = [(src, dst, r0, c0)
                  for src, dst in ((win_hbm, win_ref), (pw_hbm, pw_ref), (wout_hbm, wout_ref))
                  for r0 in range(0, src.shape[0], TS)
                  for c0 in range(0, src.shape[1], GROUP_W)]
        planes = ([prest_buf.at[p] for p in range(NGROUP)]
                  + [stage_buf.at[p] for p in range(STAGE_PLANES)])
        ahead = len(planes) - 1

        def block_copy(i):
            src, _, r0, c0 = blocks[i]
            slot = i % len(planes)
            return pltpu.make_async_copy(src.at[r0:r0 + TS, c0:c0 + GROUP_W],
                                         planes[slot], wsem.at[slot])

        for i in range(ahead):
            block_copy(i).start()
        for i, (_, dst, r0, c0) in enumerate(blocks):
            block_copy(i).wait()
            if i + ahead < len(blocks):
                block_copy(i + ahead).start()
            dst[r0:r0 + TS, c0:c0 + GROUP_W] = planes[i % len(planes)][...].astype(BF16)

    @pl.when(t == 0)
    def _reset():
        state[...] = jnp.zeros_like(state)
        a_buf[:, 0:HALO, :] = jnp.zeros((NSTRIP, HALO, LANES), F32)

    shift = mod_ref[0, 0:1, :]
    gain = ng_ref[...] * (1.0 + mod_ref[0, 1:2, :])
    gate = mod_ref[0, 2:3, :]

    for rs in row_blocks:
        xb = x_ref[0, rs, :]
        ms = jnp.mean(xb * xb, axis=-1, keepdims=True)
        u_buf[rs, :] = (xb * lax.rsqrt(ms + EPS) * gain + shift).astype(BF16)

    def glu_and_shift(j, pab):
        pa_all, pb_all = pab
        for e in range(STRIPS_PER_SLICE):
            s = j * STRIPS_PER_SLICE + e
            pa = pa_all[:, e * LANES:(e + 1) * LANES]
            pb = pb_all[:, e * LANES:(e + 1) * LANES]
            a_buf[s, HALO:HALO + TS, :] = pa * _sigmoid(pb)
            for res in range(1, SUBLANES):
                sh_buf[s, res - 1, :, :] = a_buf[s, res:res + SH_ROWS, :]

    u_all = u_buf[...]

    def proj(c0, width):
        return jnp.dot(u_all, win_ref[:, c0:c0 + width], preferred_element_type=F32)

    def proj_ab(j):
        return proj(COL_A + j * MXU_N, MXU_N), proj(COL_B + j * MXU_N, MXU_N)

    pab_prev = proj_ab(0)
    for j in range(1, NSLICE):
        pab_next = proj_ab(j)
        glu_and_shift(j - 1, pab_prev)
        pab_prev = pab_next
    glu_and_shift(NSLICE - 1, pab_prev)

    def zero_from(v):
        bits = pltpu.bitcast(v, jnp.uint32)
        sh16 = jnp.uint32(16)
        return pltpu.bitcast(lax.shift_right_logical(lax.shift_right_logical(bits, sh16), sh16), F32)

    dyn_zero = jnp.minimum(t, 0)
    nrep = CRB // SUBLANES

    def conv_strip(s_static, token):
        s = s_static + dyn_zero
        wks = None
        for bi, r in enumerate(range(0, TS, CRB)):
            z = zero_from(token)
            cb8 = jnp.broadcast_to(cb_ref[s], (SUBLANES, LANES)) + z
            acc = jnp.concatenate([cb8] * nrep, axis=0)
            if bi % TAP_GATE_EVERY == 0:
                wks = [jnp.concatenate(
                    [jnp.broadcast_to(cw_ref[s, k:k + 1, :], (SUBLANES, LANES)) + z] * nrep, axis=0)
                    for k in range(CONV_W)]
            for res in range(SUBLANES):
                taps = [(k, off // SUBLANES) for k, off in enumerate(TAP_OFF)
                        if off % SUBLANES == res]
                span = CRB + SUBLANES * max(q for _, q in taps)
                if res == 0:
                    win = a_buf[s, r:r + span, :]
                else:
                    win = sh_buf[s, res - 1, r:r + span, :]
                for k, q in taps:
                    acc = acc + win[q * SUBLANES:q * SUBLANES + CRB, :] * wks[k]
            conv_buf[s, r:r + CRB, :] = acc
            token = acc[0:SUBLANES, :]
        return token

    token = x_ref[0, 0:SUBLANES, 0:LANES]
    for g in range(NGROUP_LOOP):
        prest_buf[g] = proj(GROUP_COL[g], GROUP_W)
        for e in range(STRIPS_PER_GROUP):
            token = conv_strip(g * STRIPS_PER_GROUP + e, token)
    prest_buf[G_RG] = proj(GROUP_COL[G_RG], GROUP_W)
    a_buf[:, 0:HALO, :] = a_buf[:, TS:TS + HALO, :]

    lng = lng_ref[...]
    lnb = lnb_ref[...]
    for rs in row_blocks:
        cv = jnp.concatenate([conv_buf[s, rs, :] for s in range(NSTRIP)], axis=-1)
        mu = jnp.mean(cv, axis=-1, keepdims=True)
        cen = cv - mu
        var = jnp.mean(cen * cen, axis=-1, keepdims=True)
        ln = cen * lax.rsqrt(var + EPS) * lng + lnb
        c_buf[rs, :] = _silu(ln).astype(BF16)

    pwo = jnp.dot(c_buf[...], pw_ref[...], preferred_element_type=F32)
    for rs in row_blocks:
        ycat_buf[rs, 0:D_CONV] = (pwo[rs, :] * _silu(prest_buf[G_GATE, rs, :])).astype(BF16)

    half = DK // 2
    kscale = DK ** -0.5
    for rs in row_blocks:
        cs = cos_ref[rs, :]
        sn = sin_ref[rs, :]
        for h in range(HEADS):
            lo = slice(h * DK, h * DK + half)
            hi = slice(h * DK + half, (h + 1) * DK)
            q1 = prest_buf[G_Q, rs, lo]
            q2 = prest_buf[G_Q, rs, hi]
            q_buf[rs, lo] = (q1 * cs - q2 * sn).astype(BF16)
            q_buf[rs, hi] = (q1 * sn + q2 * cs).astype(BF16)
            k1 = prest_buf[G_K, rs, lo]
            k2 = prest_buf[G_K, rs, hi]
            o1 = (k1 * cs - k2 * sn) * kscale
            o2 = (k1 * sn + k2 * cs) * kscale
            k_buf[rs, lo] = o1.astype(BF16)
            k_buf[rs, hi] = o2.astype(BF16)
            kdr = kd_ref[h, rs, :]
            kdec_buf[rs, lo] = (o1 * kdr).astype(BF16)
            kdec_buf[rs, hi] = (o2 * kdr).astype(BF16)
        v_buf[rs, :] = prest_buf[G_V, rs, :].astype(BF16)

    for h in range(HEADS):
        hs = slice(h * DK, (h + 1) * DK)
        q = q_buf[:, hs]
        v = v_buf[:, hs]
        st = state[h]
        scores = lax.dot_general(q, k_buf[:, hs], (((1,), (1,)), ((), ())),
                                 preferred_element_type=F32) * mask_ref[h]
        inner = jnp.dot(scores.astype(BF16), v, preferred_element_type=F32)
        qd = qd_ref[h]
        cross = jnp.dot(q, st.astype(BF16), preferred_element_type=F32)
        upd = lax.dot_general(kdec_buf[:, hs], v, (((0,), (0,)), ((), ())),
                              preferred_element_type=F32)
        state[h] = st * cd_ref[h] + upd
        o = inner + cross * jnp.concatenate([qd, qd], axis=-1)
        mu = jnp.mean(o, axis=-1, keepdims=True)
        cen = o - mu
        var = jnp.mean(cen * cen, axis=-1, keepdims=True)
        gn = cen * lax.rsqrt(var + EPS) * gng_ref[:, hs] + gnb_ref[:, hs]
        rg = prest_buf[G_RG, :, hs]
        ycat_buf[:, D_CONV + h * DV:D_CONV + (h + 1) * DV] = (gn * _silu(rg)).astype(BF16)

    yo = jnp.dot(ycat_buf[...], wout_ref[...], preferred_element_type=F32)
    fg = fg_ref[...]
    for rs in row_blocks:
        hres = x_ref[0, rs, :] + gate * yo[rs, :]
        ms = jnp.mean(hres * hres, axis=-1, keepdims=True)
        out_ref[0, rs, :] = hres * lax.rsqrt(ms + EPS) * fg


def _retention_tables():
    log_g = np.log(1.0 - np.exp2(-5.0 - np.arange(HEADS, dtype=np.float64)))
    idx = np.arange(TS, dtype=np.float64)
    diff = idx[:, None] - idx[None, :]
    mask = np.where(diff >= 0, np.exp(log_g[:, None, None] * np.maximum(diff, 0.0)[None]), 0.0)
    qd = np.exp(log_g[:, None] * (idx + 1.0)[None, :])
    kd = np.exp(log_g[:, None] * (TS - 1.0 - idx)[None, :])
    cd = np.exp(log_g * TS)
    qd = np.broadcast_to(qd[:, :, None], (HEADS, TS, DV // 2))
    kd = np.broadcast_to(kd[:, :, None], (HEADS, TS, DK // 2))
    cd = np.broadcast_to(cd[:, None, None], (HEADS, 1, DV))
    return tuple(jnp.asarray(np.ascontiguousarray(a), dtype=F32) for a in (mask, qd, kd, cd))


def _rope_tables(seq):
    inv_freq = 1.0 / (ROPE_BASE ** np.linspace(0.0, 1.0, DK // 2, dtype=np.float64))
    theta = np.arange(seq, dtype=np.float64)[:, None] * inv_freq[None, :]
    return jnp.asarray(np.cos(theta), dtype=F32), jnp.asarray(np.sin(theta), dtype=F32)


def _const_spec(shape):
    nd = len(shape)
    return pl.BlockSpec(shape, lambda b, t: (0,) * nd, pipeline_mode=pl.Buffered(1))


def kernel(x, c, ada_w, ada_b, norm_g, w_in, conv_w, conv_b, conv_ln_g, conv_ln_b,
           conv_pw, ret_gn_g, ret_gn_b, w_out, final_g):
    bsz, seq, _ = x.shape
    assert seq % TS == 0 and TS % RB == 0 and TS % CRB == 0

    mod = _adaln_mod(c, ada_w[0], ada_b[0]).reshape(bsz, 3, D_MODEL)
    cos, sin = _rope_tables(seq)
    mask, qd, kd, cd = _retention_tables()
    cw = conv_w[0].reshape(CONV_W, NSTRIP, LANES).transpose(1, 0, 2)
    cb = conv_b[0].reshape(NSTRIP, 1, LANES)
    row = lambda a: a.reshape(1, -1).astype(F32)

    operands = [
        (x, pl.BlockSpec((1, TS, D_MODEL), lambda b, t: (b, t, 0))),
        (mod, pl.BlockSpec((1, 3, D_MODEL), lambda b, t: (b, 0, 0))),
        (row(norm_g[0]), _const_spec((1, D_MODEL))),
        (w_in[0], pl.BlockSpec(memory_space=pl.ANY)),
        (cw, _const_spec((NSTRIP, CONV_W, LANES))),
        (cb, _const_spec((NSTRIP, 1, LANES))),
        (row(conv_ln_g[0]), _const_spec((1, D_CONV))),
        (row(conv_ln_b[0]), _const_spec((1, D_CONV))),
        (conv_pw[0], pl.BlockSpec(memory_space=pl.ANY)),
        (row(ret_gn_g[0]), _const_spec((1, D_RET))),
        (row(ret_gn_b[0]), _const_spec((1, D_RET))),
        (w_out[0], pl.BlockSpec(memory_space=pl.ANY)),
        (row(final_g), _const_spec((1, D_MODEL))),
        (cos, pl.BlockSpec((TS, DK // 2), lambda b, t: (t, 0))),
        (sin, pl.BlockSpec((TS, DK // 2), lambda b, t: (t, 0))),
        (mask, _const_spec((HEADS, TS, TS))),
        (qd, _const_spec((HEADS, TS, DV // 2))),
        (kd, _const_spec((HEADS, TS, DK // 2))),
        (cd, _const_spec((HEADS, 1, DV))),
    ]
    arrays = [a for a, _ in operands]
    in_specs = [s for _, s in operands]

    scratch = [
        pltpu.VMEM((TS, D_MODEL), BF16),
        pltpu.VMEM((NSTRIP, TS + HALO, LANES), F32),
        pltpu.VMEM((NSTRIP, SUBLANES - 1, SH_ROWS, LANES), F32),
        pltpu.VMEM((NGROUP, TS, GROUP_W), F32),
        pltpu.VMEM((NSTRIP, TS, LANES), F32),
        pltpu.VMEM((TS, D_CONV), BF16),
        pltpu.VMEM((TS, HEADS * DK), BF16),
        pltpu.VMEM((TS, HEADS * DK), BF16),
        pltpu.VMEM((TS, HEADS * DK), BF16),
        pltpu.VMEM((TS, D_RET), BF16),
        pltpu.VMEM((TS, D_CONV + D_RET), BF16),
        pltpu.VMEM((HEADS, DK, DV), F32),
        pltpu.VMEM((D_MODEL, N_IN), BF16),
        pltpu.VMEM((D_CONV, D_CONV), BF16),
        pltpu.VMEM((D_CONV + D_RET, D_MODEL), BF16),
        pltpu.VMEM((STAGE_PLANES, TS, GROUP_W), F32),
        pltpu.SemaphoreType.DMA((NGROUP + STAGE_PLANES,)),
    ]

    return pl.pallas_call(
        _layer_kernel,
        out_shape=jax.ShapeDtypeStruct((bsz, seq, D_MODEL), x.dtype),
        grid=(bsz, seq // TS),
        in_specs=in_specs,
        out_specs=pl.BlockSpec((1, TS, D_MODEL), lambda b, t: (b, t, 0)),
        scratch_shapes=scratch,
        compiler_params=pltpu.CompilerParams(
            dimension_semantics=("arbitrary", "arbitrary"),
            vmem_limit_bytes=VMEM_LIMIT),
        name="hybrid_layer",
    )(*arrays)
```

```python
import numpy as np
import jax
import jax.numpy as jnp
from jax import lax
from jax.experimental import pallas as pl
from jax.experimental.pallas import tpu as pltpu

D_MODEL = 1024
D_CONV = 1024
D_RET = 1024
HEADS = 4
DK = 256
DV = 256
CONV_W = 31
ROPE_BASE = 10000.0
EPS = 1e-6
N_IN = 3 * D_CONV + 2 * HEADS * DK + 2 * D_RET

SUBLANES = 8
LANES = 128
MXU_N = 256
TS = 256
HALO = 32
RB = 16
CRB = 32
TAP_GATE_EVERY = 2
MOD_TN = 1024
STAGE_PLANES = 6
STAGE_PAIR = 2
VMEM_LIMIT = 56 * 1024 * 1024

NSLICE = D_CONV // MXU_N
GROUP_W = D_CONV
G_Q, G_K, G_V, G_GATE, G_RG = range(5)
NGROUP = 5
COL_A, COL_B = 0, D_CONV
GROUP_COL = {G_GATE: 2 * D_CONV, G_Q: 3 * D_CONV, G_K: 4 * D_CONV, G_V: 5 * D_CONV,
             G_RG: 6 * D_CONV}
NGROUP_LOOP = 4
NSTRIP = D_CONV // LANES
STRIPS_PER_SLICE = NSTRIP // NSLICE
STRIPS_PER_GROUP = NSTRIP // NGROUP_LOOP
assert GROUP_W == HEADS * DK == D_RET

TAP_OFF = [k + HALO - (CONV_W - 1) for k in range(CONV_W)]
SH_ROWS = TS + HALO - SUBLANES

NEG_LOG2E = -1.4426950408889634

F32 = jnp.float32
BF16 = jnp.bfloat16


def _sigmoid(v):
    return 1.0 / (1.0 + jnp.exp2(v * NEG_LOG2E))


def _silu(v):
    return v * _sigmoid(v)


def _mod_kernel(c_ref, w_ref, b_ref, o_ref):
    c_act = _silu(c_ref[...])
    o_ref[...] = jnp.dot(c_act, w_ref[...], preferred_element_type=F32) + b_ref[...]


def _adaln_mod(c, ada_w, ada_b):
    bsz = c.shape[0]
    n = ada_w.shape[1]
    return pl.pallas_call(
        _mod_kernel,
        out_shape=jax.ShapeDtypeStruct((bsz, n), F32),
        grid=(n // MOD_TN,),
        in_specs=[
            pl.BlockSpec((bsz, D_MODEL), lambda j: (0, 0)),
            pl.BlockSpec((D_MODEL, MOD_TN), lambda j: (0, j)),
            pl.BlockSpec((1, MOD_TN), lambda j: (0, j)),
        ],
        out_specs=pl.BlockSpec((bsz, MOD_TN), lambda j: (0, j)),
        name="adaln_mod",
    )(c, ada_w, ada_b.reshape(1, n))


def _layer_kernel(x_ref, mod_ref, ng_ref, win_hbm, cw_ref, cb_ref, lng_ref, lnb_ref,
                  pw_hbm, gng_ref, gnb_ref, wout_hbm, fg_ref, cos_ref, sin_ref,
                  mask_ref, qd_ref, kd_ref, cd_ref,
                  out_ref,
                  u_buf, a_buf, sh_buf, prest_buf, conv_buf, c_buf,
                  q_buf, kdec_buf, k_buf, v_buf, ycat_buf, state,
                  win_ref, pw_ref, wout_ref, stage_buf, wsem):
    t = pl.program_id(1)
    row_blocks = [slice(i * RB, (i + 1) * RB) for i in range(TS // RB)]

    @pl.when((t == 0) & (pl.program_id(0) == 0))
    def _load_weights():
        blocks = [(src, dst, rb, c0)
                  for src, dst in ((win_hbm, win_ref), (pw_hbm, pw_ref), (wout_hbm, wout_ref))
                  for rb in range(0, src.shape[0], STAGE_PAIR)
                  for c0 in range(0, src.shape[2], GROUP_W)]
        slots = ([prest_buf.at[p:p + STAGE_PAIR]
                  for p in range(0, NGROUP - STAGE_PAIR + 1, STAGE_PAIR)]
                 + [stage_buf.at[p:p + STAGE_PAIR]
                    for p in range(0, STAGE_PLANES - STAGE_PAIR + 1, STAGE_PAIR)])
        ahead = len(slots) - 1

        def block_copy(i):
            src, _, rb, c0 = blocks[i]
            slot = i % len(slots)
            return pltpu.make_async_copy(src.at[rb:rb + STAGE_PAIR, :, c0:c0 + GROUP_W],
                                         slots[slot], wsem.at[slot])

        for i in range(ahead):
            block_copy(i).start()
        for i, (_, dst, rb, c0) in enumerate(blocks):
            block_copy(i).wait()
            if i + ahead < len(blocks):
                block_copy(i + ahead).start()
            for j in range(STAGE_PAIR):
                r0 = (rb + j) * TS
                dst[r0:r0 + TS, c0:c0 + GROUP_W] = slots[i % len(slots)][j].astype(BF16)

    @pl.when(t == 0)
    def _reset():
        state[...] = jnp.zeros_like(state)
        a_buf[:, 0:HALO, :] = jnp.zeros((NSTRIP, HALO, LANES), F32)

    shift = mod_ref[0, 0:1, :]
    gain = ng_ref[...] * (1.0 + mod_ref[0, 1:2, :])
    gate = mod_ref[0, 2:3, :]

    for rs in row_blocks:
        xb = x_ref[0, rs, :]
        ms = jnp.mean(xb * xb, axis=-1, keepdims=True)
        u_buf[rs, :] = (xb * lax.rsqrt(ms + EPS) * gain + shift).astype(BF16)

    def glu_and_shift(j, pab):
        pa_all, pb_all = pab
        for e in range(STRIPS_PER_SLICE):
            s = j * STRIPS_PER_SLICE + e
            pa = pa_all[:, e * LANES:(e + 1) * LANES]
            pb = pb_all[:, e * LANES:(e + 1) * LANES]
            a_buf[s, HALO:HALO + TS, :] = pa * _sigmoid(pb)
            for res in range(1, SUBLANES):
                sh_buf[s, res - 1, :, :] = a_buf[s, res:res + SH_ROWS, :]

    u_all = u_buf[...]

    def proj(c0, width):
        return jnp.dot(u_all, win_ref[:, c0:c0 + width], preferred_element_type=F32)

    def proj_ab(j):
        return proj(COL_A + j * MXU_N, MXU_N), proj(COL_B + j * MXU_N, MXU_N)

    pab_prev = proj_ab(0)
    for j in range(1, NSLICE):
        pab_next = proj_ab(j)
        glu_and_shift(j - 1, pab_prev)
        pab_prev = pab_next
    glu_and_shift(NSLICE - 1, pab_prev)

    def zero_from(v):
        bits = pltpu.bitcast(v, jnp.uint32)
        sh16 = jnp.uint32(16)
        return pltpu.bitcast(lax.shift_right_logical(lax.shift_right_logical(bits, sh16), sh16), F32)

    dyn_zero = jnp.minimum(t, 0)
    nrep = CRB // SUBLANES

    def conv_strip(s_static, token):
        s = s_static + dyn_zero
        wks = None
        for bi, r in enumerate(range(0, TS, CRB)):
            z = zero_from(token)
            cb8 = jnp.broadcast_to(cb_ref[s], (SUBLANES, LANES)) + z
            acc = jnp.concatenate([cb8] * nrep, axis=0)
            if bi % TAP_GATE_EVERY == 0:
                wks = [jnp.concatenate(
                    [jnp.broadcast_to(cw_ref[s, k:k + 1, :], (SUBLANES, LANES)) + z] * nrep, axis=0)
                    for k in range(CONV_W)]
            for res in range(SUBLANES):
                taps = [(k, off // SUBLANES) for k, off in enumerate(TAP_OFF)
                        if off % SUBLANES == res]
                span = CRB + SUBLANES * max(q for _, q in taps)
                if res == 0:
                    win = a_buf[s, r:r + span, :]
                else:
                    win = sh_buf[s, res - 1, r:r + span, :]
                for k, q in taps:
                    acc = acc + win[q * SUBLANES:q * SUBLANES + CRB, :] * wks[k]
            conv_buf[s, r:r + CRB, :] = acc
            token = acc[0:SUBLANES, :]
        return token

    token = x_ref[0, 0:SUBLANES, 0:LANES]
    for g in range(NGROUP_LOOP):
        prest_buf[g] = proj(GROUP_COL[g], GROUP_W)
        for e in range(STRIPS_PER_GROUP):
            token = conv_strip(g * STRIPS_PER_GROUP + e, token)
    prest_buf[G_RG] = proj(GROUP_COL[G_RG], GROUP_W)
    a_buf[:, 0:HALO, :] = a_buf[:, TS:TS + HALO, :]

    lng = lng_ref[...]
    lnb = lnb_ref[...]
    for rs in row_blocks:
        cv = jnp.concatenate([conv_buf[s, rs, :] for s in range(NSTRIP)], axis=-1)
        mu = jnp.mean(cv, axis=-1, keepdims=True)
        cen = cv - mu
        var = jnp.mean(cen * cen, axis=-1, keepdims=True)
        ln = cen * lax.rsqrt(var + EPS) * lng + lnb
        c_buf[rs, :] = _silu(ln).astype(BF16)

    pwo = jnp.dot(c_buf[...], pw_ref[...], preferred_element_type=F32)
    for rs in row_blocks:
        ycat_buf[rs, 0:D_CONV] = (pwo[rs, :] * _silu(prest_buf[G_GATE, rs, :])).astype(BF16)

    half = DK // 2
    kscale = DK ** -0.5
    for rs in row_blocks:
        cs = cos_ref[rs, :]
        sn = sin_ref[rs, :]
        for h in range(HEADS):
            lo = slice(h * DK, h * DK + half)
            hi = slice(h * DK + half, (h + 1) * DK)
            q1 = prest_buf[G_Q, rs, lo]
            q2 = prest_buf[G_Q, rs, hi]
            q_buf[rs, lo] = (q1 * cs - q2 * sn).astype(BF16)
            q_buf[rs, hi] = (q1 * sn + q2 * cs).astype(BF16)
            k1 = prest_buf[G_K, rs, lo]
            k2 = prest_buf[G_K, rs, hi]
            o1 = (k1 * cs - k2 * sn) * kscale
            o2 = (k1 * sn + k2 * cs) * kscale
            k_buf[rs, lo] = o1.astype(BF16)
            k_buf[rs, hi] = o2.astype(BF16)
            kdr = kd_ref[h, rs, :]
            kdec_buf[rs, lo] = (o1 * kdr).astype(BF16)
            kdec_buf[rs, hi] = (o2 * kdr).astype(BF16)
        v_buf[rs, :] = prest_buf[G_V, rs, :].astype(BF16)

    for h in range(HEADS):
        hs = slice(h * DK, (h + 1) * DK)
        q = q_buf[:, hs]
        v = v_buf[:, hs]
        st = state[h]
        scores = lax.dot_general(q, k_buf[:, hs], (((1,), (1,)), ((), ())),
                                 preferred_element_type=F32) * mask_ref[h]
        inner = jnp.dot(scores.astype(BF16), v, preferred_element_type=F32)
        qd = qd_ref[h]
        cross = jnp.dot(q, st.astype(BF16), preferred_element_type=F32)
        upd = lax.dot_general(kdec_buf[:, hs], v, (((0,), (0,)), ((), ())),
                              preferred_element_type=F32)
        state[h] = st * cd_ref[h] + upd
        o = inner + cross * jnp.concatenate([qd, qd], axis=-1)
        mu = jnp.mean(o, axis=-1, keepdims=True)
        cen = o - mu
        var = jnp.mean(cen * cen, axis=-1, keepdims=True)
        gn = cen * lax.rsqrt(var + EPS) * gng_ref[:, hs] + gnb_ref[:, hs]
        rg = prest_buf[G_RG, :, hs]
        ycat_buf[:, D_CONV + h * DV:D_CONV + (h + 1) * DV] = (gn * _silu(rg)).astype(BF16)

    yo = jnp.dot(ycat_buf[...], wout_ref[...], preferred_element_type=F32)
    fg = fg_ref[...]
    for rs in row_blocks:
        hres = x_ref[0, rs, :] + gate * yo[rs, :]
        ms = jnp.mean(hres * hres, axis=-1, keepdims=True)
        out_ref[0, rs, :] = hres * lax.rsqrt(ms + EPS) * fg


def _retention_tables():
    log_g = np.log(1.0 - np.exp2(-5.0 - np.arange(HEADS, dtype=np.float64)))
    idx = np.arange(TS, dtype=np.float64)
    diff = idx[:, None] - idx[None, :]
    mask = np.where(diff >= 0, np.exp(log_g[:, None, None] * np.maximum(diff, 0.0)[None]), 0.0)
    qd = np.exp(log_g[:, None] * (idx + 1.0)[None, :])
    kd = np.exp(log_g[:, None] * (TS - 1.0 - idx)[None, :])
    cd = np.exp(log_g * TS)
    qd = np.broadcast_to(qd[:, :, None], (HEADS, TS, DV // 2))
    kd = np.broadcast_to(kd[:, :, None], (HEADS, TS, DK // 2))
    cd = np.broadcast_to(cd[:, None, None], (HEADS, 1, DV))
    return tuple(jnp.asarray(np.ascontiguousarray(a), dtype=F32) for a in (mask, qd, kd, cd))


def _rope_tables(seq):
    inv_freq = 1.0 / (ROPE_BASE ** np.linspace(0.0, 1.0, DK // 2, dtype=np.float64))
    theta = np.arange(seq, dtype=np.float64)[:, None] * inv_freq[None, :]
    return jnp.asarray(np.cos(theta), dtype=F32), jnp.asarray(np.sin(theta), dtype=F32)


def _const_spec(shape):
    nd = len(shape)
    return pl.BlockSpec(shape, lambda b, t: (0,) * nd, pipeline_mode=pl.Buffered(1))


def kernel(x, c, ada_w, ada_b, norm_g, w_in, conv_w, conv_b, conv_ln_g, conv_ln_b,
           conv_pw, ret_gn_g, ret_gn_b, w_out, final_g):
    bsz, seq, _ = x.shape
    assert seq % TS == 0 and TS % RB == 0 and TS % CRB == 0

    mod = _adaln_mod(c, ada_w[0], ada_b[0]).reshape(bsz, 3, D_MODEL)
    cos, sin = _rope_tables(seq)
    mask, qd, kd, cd = _retention_tables()
    cw = conv_w[0].reshape(CONV_W, NSTRIP, LANES).transpose(1, 0, 2)
    cb = conv_b[0].reshape(NSTRIP, 1, LANES)
    row = lambda a: a.reshape(1, -1).astype(F32)

    operands = [
        (x, pl.BlockSpec((1, TS, D_MODEL), lambda b, t: (b, t, 0))),
        (mod, pl.BlockSpec((1, 3, D_MODEL), lambda b, t: (b, 0, 0))),
        (row(norm_g[0]), _const_spec((1, D_MODEL))),
        (w_in[0].reshape(D_MODEL // TS, TS, N_IN), pl.BlockSpec(memory_space=pl.ANY)),
        (cw, _const_spec((NSTRIP, CONV_W, LANES))),
        (cb, _const_spec((NSTRIP, 1, LANES))),
        (row(conv_ln_g[0]), _const_spec((1, D_CONV))),
        (row(conv_ln_b[0]), _const_spec((1, D_CONV))),
        (conv_pw[0].reshape(D_CONV // TS, TS, D_CONV), pl.BlockSpec(memory_space=pl.ANY)),
        (row(ret_gn_g[0]), _const_spec((1, D_RET))),
        (row(ret_gn_b[0]), _const_spec((1, D_RET))),
        (w_out[0].reshape((D_CONV + D_RET) // TS, TS, D_MODEL), pl.BlockSpec(memory_space=pl.ANY)),
        (row(final_g), _const_spec((1, D_MODEL))),
        (cos, pl.BlockSpec((TS, DK // 2), lambda b, t: (t, 0))),
        (sin, pl.BlockSpec((TS, DK // 2), lambda b, t: (t, 0))),
        (mask, _const_spec((HEADS, TS, TS))),
        (qd, _const_spec((HEADS, TS, DV // 2))),
        (kd, _const_spec((HEADS, TS, DK // 2))),
        (cd, _const_spec((HEADS, 1, DV))),
    ]
    arrays = [a for a, _ in operands]
    in_specs = [s for _, s in operands]

    scratch = [
        pltpu.VMEM((TS, D_MODEL), BF16),
        pltpu.VMEM((NSTRIP, TS + HALO, LANES), F32),
        pltpu.VMEM((NSTRIP, SUBLANES - 1, SH_ROWS, LANES), F32),
        pltpu.VMEM((NGROUP, TS, GROUP_W), F32),
        pltpu.VMEM((NSTRIP, TS, LANES), F32),
        pltpu.VMEM((TS, D_CONV), BF16),
        pltpu.VMEM((TS, HEADS * DK), BF16),
        pltpu.VMEM((TS, HEADS * DK), BF16),
        pltpu.VMEM((TS, HEADS * DK), BF16),
        pltpu.VMEM((TS, D_RET), BF16),
        pltpu.VMEM((TS, D_CONV + D_RET), BF16),
        pltpu.VMEM((HEADS, DK, DV), F32),
        pltpu.VMEM((D_MODEL, N_IN), BF16),
        pltpu.VMEM((D_CONV, D_CONV), BF16),
        pltpu.VMEM((D_CONV + D_RET, D_MODEL), BF16),
        pltpu.VMEM((STAGE_PLANES, TS, GROUP_W), F32),
        pltpu.SemaphoreType.DMA((NGROUP + STAGE_PLANES,)),
    ]

    return pl.pallas_call(
        _layer_kernel,
        out_shape=jax.ShapeDtypeStruct((bsz, seq, D_MODEL), x.dtype),
        grid=(bsz, seq // TS),
        in_specs=in_specs,
        out_specs=pl.BlockSpec((1, TS, D_MODEL), lambda b, t: (b, t, 0)),
        scratch_shapes=scratch,
        compiler_params=pltpu.CompilerParams(
            dimension_semantics=("arbitrary", "arbitrary"),
            vmem_limit_bytes=VMEM_LIMIT),
        name="hybrid_layer",
    )(*arrays)
```

```python
import numpy as np
import jax
import jax.numpy as jnp
from jax import lax
from jax.experimental import pallas as pl
from jax.experimental.pallas import tpu as pltpu

D_MODEL = 1024
D_CONV = 1024
D_RET = 1024
HEADS = 4
DK = 256
DV = 256
CONV_W = 31
ROPE_BASE = 10000.0
EPS = 1e-6
N_IN = 3 * D_CONV + 2 * HEADS * DK + 2 * D_RET

SUBLANES = 8
LANES = 128
MXU_N = 256
TS = 256
HALO = 32
RB = 16
CRB = 32
TAP_GATE_EVERY = 2
MOD_TN = 1024
STAGE_PLANES = 10
STAGE_PAIR = 2
VMEM_LIMIT = 56 * 1024 * 1024

NSLICE = D_CONV // MXU_N
GROUP_W = D_CONV
G_Q, G_K, G_V, G_GATE, G_RG = range(5)
NGROUP = 5
COL_A, COL_B = 0, D_CONV
GROUP_COL = {G_GATE: 2 * D_CONV, G_Q: 3 * D_CONV, G_K: 4 * D_CONV, G_V: 5 * D_CONV,
             G_RG: 6 * D_CONV}
NGROUP_LOOP = 4
NSTRIP = D_CONV // LANES
STRIPS_PER_SLICE = NSTRIP // NSLICE
STRIPS_PER_GROUP = NSTRIP // NGROUP_LOOP
assert GROUP_W == HEADS * DK == D_RET

TAP_OFF = [k + HALO - (CONV_W - 1) for k in range(CONV_W)]
SH_ROWS = TS + HALO - SUBLANES

NEG_LOG2E = -1.4426950408889634

F32 = jnp.float32
BF16 = jnp.bfloat16


def _sigmoid(v):
    return 1.0 / (1.0 + jnp.exp2(v * NEG_LOG2E))


def _silu(v):
    return v * _sigmoid(v)


def _mod_kernel(c_ref, w_ref, b_ref, o_ref):
    c_act = _silu(c_ref[...])
    o_ref[...] = jnp.dot(c_act, w_ref[...], preferred_element_type=F32) + b_ref[...]


def _adaln_mod(c, ada_w, ada_b):
    bsz = c.shape[0]
    n = ada_w.shape[1]
    return pl.pallas_call(
        _mod_kernel,
        out_shape=jax.ShapeDtypeStruct((bsz, n), F32),
        grid=(n // MOD_TN,),
        in_specs=[
            pl.BlockSpec((bsz, D_MODEL), lambda j: (0, 0)),
            pl.BlockSpec((D_MODEL, MOD_TN), lambda j: (0, j)),
            pl.BlockSpec((1, MOD_TN), lambda j: (0, j)),
        ],
        out_specs=pl.BlockSpec((bsz, MOD_TN), lambda j: (0, j)),
        name="adaln_mod",
    )(c, ada_w, ada_b.reshape(1, n))


def _layer_kernel(x_ref, mod_ref, ng_ref, win_hbm, cw_ref, cb_ref, lng_ref, lnb_ref,
                  pw_hbm, gng_ref, gnb_ref, wout_hbm, fg_ref, cos_ref, sin_ref,
                  mask_ref, qd_ref, kd_ref, cd_ref,
                  out_ref,
                  u_buf, a_buf, sh_buf, prest_buf, conv_buf, c_buf,
                  q_buf, kdec_buf, k_buf, v_buf, ycat_buf, state,
                  win_ref, pw_ref, wout_ref, stage_buf, wsem):
    t = pl.program_id(1)
    row_blocks = [slice(i * RB, (i + 1) * RB) for i in range(TS // RB)]

    @pl.when((t == 0) & (pl.program_id(0) == 0))
    def _load_weights():
        blocks = [(src, dst, rb, c0)
                  for src, dst in ((win_hbm, win_ref), (pw_hbm, pw_ref), (wout_hbm, wout_ref))
                  for rb in range(0, src.shape[0], STAGE_PAIR)
                  for c0 in range(0, src.shape[2], GROUP_W)]
        slots = ([prest_buf.at[p:p + STAGE_PAIR]
                  for p in range(0, NGROUP - STAGE_PAIR + 1, STAGE_PAIR)]
                 + [stage_buf.at[p:p + STAGE_PAIR]
                    for p in range(0, STAGE_PLANES - STAGE_PAIR + 1, STAGE_PAIR)])
        ahead = len(slots) - 1

        def block_copy(i):
            src, _, rb, c0 = blocks[i]
            slot = i % len(slots)
            return pltpu.make_async_copy(src.at[rb:rb + STAGE_PAIR, :, c0:c0 + GROUP_W],
                                         slots[slot], wsem.at[slot])

        for i in range(ahead):
            block_copy(i).start()
        for i, (_, dst, rb, c0) in enumerate(blocks):
            block_copy(i).wait()
            if i + ahead < len(blocks):
                block_copy(i + ahead).start()
            for j in range(STAGE_PAIR):
                r0 = (rb + j) * TS
                dst[r0:r0 + TS, c0:c0 + GROUP_W] = slots[i % len(slots)][j].astype(BF16)

    @pl.when(t == 0)
    def _reset():
        state[...] = jnp.zeros_like(state)
        a_buf[:, 0:HALO, :] = jnp.zeros((NSTRIP, HALO, LANES), F32)

    shift = mod_ref[0, 0:1, :]
    gain = ng_ref[...] * (1.0 + mod_ref[0, 1:2, :])
    gate = mod_ref[0, 2:3, :]

    for rs in row_blocks:
        xb = x_ref[0, rs, :]
        ms = jnp.mean(xb * xb, axis=-1, keepdims=True)
        u_buf[rs, :] = (xb * lax.rsqrt(ms + EPS) * gain + shift).astype(BF16)

    def glu_and_shift(j, pab):
        pa_all, pb_all = pab
        for e in range(STRIPS_PER_SLICE):
            s = j * STRIPS_PER_SLICE + e
            pa = pa_all[:, e * LANES:(e + 1) * LANES]
            pb = pb_all[:, e * LANES:(e + 1) * LANES]
            a_buf[s, HALO:HALO + TS, :] = pa * _sigmoid(pb)
            for res in range(1, SUBLANES):
                sh_buf[s, res - 1, :, :] = a_buf[s, res:res + SH_ROWS, :]

    u_all = u_buf[...]

    def proj(c0, width):
        return jnp.dot(u_all, win_ref[:, c0:c0 + width], preferred_element_type=F32)

    def proj_ab(j):
        return proj(COL_A + j * MXU_N, MXU_N), proj(COL_B + j * MXU_N, MXU_N)

    pab_prev = proj_ab(0)
    for j in range(1, NSLICE):
        pab_next = proj_ab(j)
        glu_and_shift(j - 1, pab_prev)
        pab_prev = pab_next
    glu_and_shift(NSLICE - 1, pab_prev)

    def zero_from(v):
        bits = pltpu.bitcast(v, jnp.uint32)
        sh16 = jnp.uint32(16)
        return pltpu.bitcast(lax.shift_right_logical(lax.shift_right_logical(bits, sh16), sh16), F32)

    dyn_zero = jnp.minimum(t, 0)
    nrep = CRB // SUBLANES

    def conv_strip(s_static, token):
        s = s_static + dyn_zero
        wks = None
        for bi, r in enumerate(range(0, TS, CRB)):
            z = zero_from(token)
            cb8 = jnp.broadcast_to(cb_ref[s], (SUBLANES, LANES)) + z
            acc = jnp.concatenate([cb8] * nrep, axis=0)
            if bi % TAP_GATE_EVERY == 0:
                wks = [jnp.concatenate(
                    [jnp.broadcast_to(cw_ref[s, k:k + 1, :], (SUBLANES, LANES)) + z] * nrep, axis=0)
                    for k in range(CONV_W)]
            for res in range(SUBLANES):
                taps = [(k, off // SUBLANES) for k, off in enumerate(TAP_OFF)
                        if off % SUBLANES == res]
                span = CRB + SUBLANES * max(q for _, q in taps)
                if res == 0:
                    win = a_buf[s, r:r + span, :]
                else:
                    win = sh_buf[s, res - 1, r:r + span, :]
                for k, q in taps:
                    acc = acc + win[q * SUBLANES:q * SUBLANES + CRB, :] * wks[k]
            conv_buf[s, r:r + CRB, :] = acc
            token = acc[0:SUBLANES, :]
        return token

    token = x_ref[0, 0:SUBLANES, 0:LANES]
    for g in range(NGROUP_LOOP):
        prest_buf[g] = proj(GROUP_COL[g], GROUP_W)
        for e in range(STRIPS_PER_GROUP):
            token = conv_strip(g * STRIPS_PER_GROUP + e, token)
    prest_buf[G_RG] = proj(GROUP_COL[G_RG], GROUP_W)
    a_buf[:, 0:HALO, :] = a_buf[:, TS:TS + HALO, :]

    lng = lng_ref[...]
    lnb = lnb_ref[...]
    for rs in row_blocks:
        cv = jnp.concatenate([conv_buf[s, rs, :] for s in range(NSTRIP)], axis=-1)
        mu = jnp.mean(cv, axis=-1, keepdims=True)
        cen = cv - mu
        var = jnp.mean(cen * cen, axis=-1, keepdims=True)
        ln = cen * lax.rsqrt(var + EPS) * lng + lnb
        c_buf[rs, :] = _silu(ln).astype(BF16)

    pwo = jnp.dot(c_buf[...], pw_ref[...], preferred_element_type=F32)
    for rs in row_blocks:
        ycat_buf[rs, 0:D_CONV] = (pwo[rs, :] * _silu(prest_buf[G_GATE, rs, :])).astype(BF16)

    half = DK // 2
    kscale = DK ** -0.5
    for rs in row_blocks:
        cs = cos_ref[rs, :]
        sn = sin_ref[rs, :]
        for h in range(HEADS):
            lo = slice(h * DK, h * DK + half)
            hi = slice(h * DK + half, (h + 1) * DK)
            q1 = prest_buf[G_Q, rs, lo]
            q2 = prest_buf[G_Q, rs, hi]
            q_buf[rs, lo] = (q1 * cs - q2 * sn).astype(BF16)
            q_buf[rs, hi] = (q1 * sn + q2 * cs).astype(BF16)
            k1 = prest_buf[G_K, rs, lo]
            k2 = prest_buf[G_K, rs, hi]
            o1 = (k1 * cs - k2 * sn) * kscale
            o2 = (k1 * sn + k2 * cs) * kscale
            k_buf[rs, lo] = o1.astype(BF16)
            k_buf[rs, hi] = o2.astype(BF16)
            kdr = kd_ref[h, rs, :]
            kdec_buf[rs, lo] = (o1 * kdr).astype(BF16)
            kdec_buf[rs, hi] = (o2 * kdr).astype(BF16)
        v_buf[rs, :] = prest_buf[G_V, rs, :].astype(BF16)

    for h in range(HEADS):
        hs = slice(h * DK, (h + 1) * DK)
        q = q_buf[:, hs]
        v = v_buf[:, hs]
        st = state[h]
        scores = lax.dot_general(q, k_buf[:, hs], (((1,), (1,)), ((), ())),
                                 preferred_element_type=F32) * mask_ref[h]
        inner = jnp.dot(scores.astype(BF16), v, preferred_element_type=F32)
        qd = qd_ref[h]
        cross = jnp.dot(q, st.astype(BF16), preferred_element_type=F32)
        upd = lax.dot_general(kdec_buf[:, hs], v, (((0,), (0,)), ((), ())),
                              preferred_element_type=F32)
        state[h] = st * cd_ref[h] + upd
        o = inner + cross * jnp.concatenate([qd, qd], axis=-1)
        mu = jnp.mean(o, axis=-1, keepdims=True)
        cen = o - mu
        var = jnp.mean(cen * cen, axis=-1, keepdims=True)
        gn = cen * lax.rsqrt(var + EPS) * gng_ref[:, hs] + gnb_ref[:, hs]
        rg = prest_buf[G_RG, :, hs]
        ycat_buf[:, D_CONV + h * DV:D_CONV + (h + 1) * DV] = (gn * _silu(rg)).astype(BF16)

    yo = jnp.dot(ycat_buf[...], wout_ref[...], preferred_element_type=F32)
    fg = fg_ref[...]
    for rs in row_blocks:
        hres = x_ref[0, rs, :] + gate * yo[rs, :]
        ms = jnp.mean(hres * hres, axis=-1, keepdims=True)
        out_ref[0, rs, :] = hres * lax.rsqrt(ms + EPS) * fg


def _retention_tables():
    log_g = np.log(1.0 - np.exp2(-5.0 - np.arange(HEADS, dtype=np.float64)))
    idx = np.arange(TS, dtype=np.float64)
    diff = idx[:, None] - idx[None, :]
    mask = np.where(diff >= 0, np.exp(log_g[:, None, None] * np.maximum(diff, 0.0)[None]), 0.0)
    qd = np.exp(log_g[:, None] * (idx + 1.0)[None, :])
    kd = np.exp(log_g[:, None] * (TS - 1.0 - idx)[None, :])
    cd = np.exp(log_g * TS)
    qd = np.broadcast_to(qd[:, :, None], (HEADS, TS, DV // 2))
    kd = np.broadcast_to(kd[:, :, None], (HEADS, TS, DK // 2))
    cd = np.broadcast_to(cd[:, None, None], (HEADS, 1, DV))
    return tuple(jnp.asarray(np.ascontiguousarray(a), dtype=F32) for a in (mask, qd, kd, cd))


def _rope_tables(seq):
    inv_freq = 1.0 / (ROPE_BASE ** np.linspace(0.0, 1.0, DK // 2, dtype=np.float64))
    theta = np.arange(seq, dtype=np.float64)[:, None] * inv_freq[None, :]
    return jnp.asarray(np.cos(theta), dtype=F32), jnp.asarray(np.sin(theta), dtype=F32)


def _const_spec(shape):
    nd = len(shape)
    return pl.BlockSpec(shape, lambda b, t: (0,) * nd, pipeline_mode=pl.Buffered(1))


def kernel(x, c, ada_w, ada_b, norm_g, w_in, conv_w, conv_b, conv_ln_g, conv_ln_b,
           conv_pw, ret_gn_g, ret_gn_b, w_out, final_g):
    bsz, seq, _ = x.shape
    assert seq % TS == 0 and TS % RB == 0 and TS % CRB == 0

    mod = _adaln_mod(c, ada_w[0], ada_b[0]).reshape(bsz, 3, D_MODEL)
    cos, sin = _rope_tables(seq)
    mask, qd, kd, cd = _retention_tables()
    cw = conv_w[0].reshape(CONV_W, NSTRIP, LANES).transpose(1, 0, 2)
    cb = conv_b[0].reshape(NSTRIP, 1, LANES)
    row = lambda a: a.reshape(1, -1).astype(F32)

    operands = [
        (x, pl.BlockSpec((1, TS, D_MODEL), lambda b, t: (b, t, 0))),
        (mod, pl.BlockSpec((1, 3, D_MODEL), lambda b, t: (b, 0, 0))),
        (row(norm_g[0]), _const_spec((1, D_MODEL))),
        (w_in[0].reshape(D_MODEL // TS, TS, N_IN), pl.BlockSpec(memory_space=pl.ANY)),
        (cw, _const_spec((NSTRIP, CONV_W, LANES))),
        (cb, _const_spec((NSTRIP, 1, LANES))),
        (row(conv_ln_g[0]), _const_spec((1, D_CONV))),
        (row(conv_ln_b[0]), _const_spec((1, D_CONV))),
        (conv_pw[0].reshape(D_CONV // TS, TS, D_CONV), pl.BlockSpec(memory_space=pl.ANY)),
        (row(ret_gn_g[0]), _const_spec((1, D_RET))),
        (row(ret_gn_b[0]), _const_spec((1, D_RET))),
        (w_out[0].reshape((D_CONV + D_RET) // TS, TS, D_MODEL), pl.BlockSpec(memory_space=pl.ANY)),
        (row(final_g), _const_spec((1, D_MODEL))),
        (cos, pl.BlockSpec((TS, DK // 2), lambda b, t: (t, 0))),
        (sin, pl.BlockSpec((TS, DK // 2), lambda b, t: (t, 0))),
        (mask, _const_spec((HEADS, TS, TS))),
        (qd, _const_spec((HEADS, TS, DV // 2))),
        (kd, _const_spec((HEADS, TS, DK // 2))),
        (cd, _const_spec((HEADS, 1, DV))),
    ]
    arrays = [a for a, _ in operands]
    in_specs = [s for _, s in operands]

    scratch = [
        pltpu.VMEM((TS, D_MODEL), BF16),
        pltpu.VMEM((NSTRIP, TS + HALO, LANES), F32),
        pltpu.VMEM((NSTRIP, SUBLANES - 1, SH_ROWS, LANES), F32),
        pltpu.VMEM((NGROUP, TS, GROUP_W), F32),
        pltpu.VMEM((NSTRIP, TS, LANES), F32),
        pltpu.VMEM((TS, D_CONV), BF16),
        pltpu.VMEM((TS, HEADS * DK), BF16),
        pltpu.VMEM((TS, HEADS * DK), BF16),
        pltpu.VMEM((TS, HEADS * DK), BF16),
        pltpu.VMEM((TS, D_RET), BF16),
        pltpu.VMEM((TS, D_CONV + D_RET), BF16),
        pltpu.VMEM((HEADS, DK, DV), F32),
        pltpu.VMEM((D_MODEL, N_IN), BF16),
        pltpu.VMEM((D_CONV, D_CONV), BF16),
        pltpu.VMEM((D_CONV + D_RET, D_MODEL), BF16),
        pltpu.VMEM((STAGE_PLANES, TS, GROUP_W), F32),
        pltpu.SemaphoreType.DMA((NGROUP + STAGE_PLANES,)),
    ]

    return pl.pallas_call(
        _layer_kernel,
        out_shape=jax.ShapeDtypeStruct((bsz, seq, D_MODEL), x.dtype),
        grid=(bsz, seq // TS),
        in_specs=in_specs,
        out_specs=pl.BlockSpec((1, TS, D_MODEL), lambda b, t: (b, t, 0)),
        scratch_shapes=scratch,
        compiler_params=pltpu.CompilerParams(
            dimension_semantics=("arbitrary", "arbitrary"),
            vmem_limit_bytes=VMEM_LIMIT),
        name="hybrid_layer",
    )(*arrays)
```

```python
import numpy as np
import jax
import jax.numpy as jnp
from jax import lax
from jax.experimental import pallas as pl
from jax.experimental.pallas import tpu as pltpu

D_MODEL = 1024
D_CONV = 1024
D_RET = 1024
HEADS = 4
DK = 256
DV = 256
CONV_W = 31
ROPE_BASE = 10000.0
EPS = 1e-6
N_IN = 3 * D_CONV + 2 * HEADS * DK + 2 * D_RET

SUBLANES = 8
LANES = 128
MXU_N = 256
TS = 256
HALO = 32
RB = 16
CRB = 32
TAP_GATE_EVERY = 2
MOD_TN = 1024
STAGE_PLANES = 6
STAGE_PAIR = 2
VMEM_LIMIT = 56 * 1024 * 1024

NSLICE = D_CONV // MXU_N
GROUP_W = D_CONV
G_Q, G_K, G_V, G_GATE, G_RG = range(5)
NGROUP = 5
COL_A, COL_B = 0, D_CONV
GROUP_COL = {G_GATE: 2 * D_CONV, G_Q: 3 * D_CONV, G_K: 4 * D_CONV, G_V: 5 * D_CONV,
             G_RG: 6 * D_CONV}
NGROUP_LOOP = 4
NSTRIP = D_CONV // LANES
STRIPS_PER_SLICE = NSTRIP // NSLICE
STRIPS_PER_GROUP = NSTRIP // NGROUP_LOOP
assert GROUP_W == HEADS * DK == D_RET

TAP_OFF = [k + HALO - (CONV_W - 1) for k in range(CONV_W)]
SH_ROWS = TS + HALO - SUBLANES

NEG_LOG2E = -1.4426950408889634

F32 = jnp.float32
BF16 = jnp.bfloat16


def _sigmoid(v):
    return 1.0 / (1.0 + jnp.exp2(v * NEG_LOG2E))


def _silu(v):
    return v * _sigmoid(v)


def _mod_kernel(c_ref, w_ref, b_ref, o_ref):
    c_act = _silu(c_ref[...])
    o_ref[...] = jnp.dot(c_act, w_ref[...], preferred_element_type=F32) + b_ref[...]


def _adaln_mod(c, ada_w, ada_b):
    bsz = c.shape[0]
    n = ada_w.shape[1]
    return pl.pallas_call(
        _mod_kernel,
        out_shape=jax.ShapeDtypeStruct((bsz, n), F32),
        grid=(n // MOD_TN,),
        in_specs=[
            pl.BlockSpec((bsz, D_MODEL), lambda j: (0, 0)),
            pl.BlockSpec((D_MODEL, MOD_TN), lambda j: (0, j)),
            pl.BlockSpec((1, MOD_TN), lambda j: (0, j)),
        ],
        out_specs=pl.BlockSpec((bsz, MOD_TN), lambda j: (0, j)),
        name="adaln_mod",
    )(c, ada_w, ada_b.reshape(1, n))


def _layer_kernel(x_ref, mod_ref, ng_ref, win_hbm, cw_ref, cb_ref, lng_ref, lnb_ref,
                  pw_hbm, gng_ref, gnb_ref, wout_hbm, fg_ref, cos_ref, sin_ref,
                  mask_ref, qd_ref, kd_ref, cd_ref,
                  out_ref,
                  u_buf, a_buf, sh_buf, prest_buf, conv_buf, c_buf,
                  q_buf, kdec_buf, k_buf, v_buf, ycat_buf, state,
                  win_ref, pw_ref, wout_ref, stage_buf, wsem):
    t = pl.program_id(1)
    row_blocks = [slice(i * RB, (i + 1) * RB) for i in range(TS // RB)]

    @pl.when((t == 0) & (pl.program_id(0) == 0))
    def _load_weights():
        blocks = [(src, dst, rb, c0)
                  for src, dst in ((win_hbm, win_ref), (pw_hbm, pw_ref), (wout_hbm, wout_ref))
                  for rb in range(0, src.shape[0], STAGE_PAIR)
                  for c0 in range(0, src.shape[2], GROUP_W)]
        slots = ([prest_buf.at[p:p + STAGE_PAIR]
                  for p in range(0, NGROUP - STAGE_PAIR + 1, STAGE_PAIR)]
                 + [stage_buf.at[p:p + STAGE_PAIR]
                    for p in range(0, STAGE_PLANES - STAGE_PAIR + 1, STAGE_PAIR)])
        ahead = len(slots) - 1

        def block_copy(i):
            src, _, rb, c0 = blocks[i]
            slot = i % len(slots)
            return pltpu.make_async_copy(src.at[rb:rb + STAGE_PAIR, :, c0:c0 + GROUP_W],
                                         slots[slot], wsem.at[slot])

        for i in range(ahead):
            block_copy(i).start()
        for i, (_, dst, rb, c0) in enumerate(blocks):
            block_copy(i).wait()
            if i + ahead < len(blocks):
                block_copy(i + ahead).start()
            for j in range(STAGE_PAIR):
                r0 = (rb + j) * TS
                dst[r0:r0 + TS, c0:c0 + GROUP_W] = slots[i % len(slots)][j].astype(BF16)

    @pl.when(t == 0)
    def _reset():
        state[...] = jnp.zeros_like(state)
        a_buf[:, 0:HALO, :] = jnp.zeros((NSTRIP, HALO, LANES), F32)

    shift = mod_ref[0, 0:1, :]
    gain = ng_ref[...] * (1.0 + mod_ref[0, 1:2, :])
    gate = mod_ref[0, 2:3, :]

    for rs in row_blocks:
        xb = x_ref[0, rs, :]
        ms = jnp.mean(xb * xb, axis=-1, keepdims=True)
        u_buf[rs, :] = (xb * lax.rsqrt(ms + EPS) * gain + shift).astype(BF16)

    def glu_and_shift(j, pab):
        pa_all, pb_all = pab
        for e in range(STRIPS_PER_SLICE):
            s = j * STRIPS_PER_SLICE + e
            pa = pa_all[:, e * LANES:(e + 1) * LANES]
            pb = pb_all[:, e * LANES:(e + 1) * LANES]
            a_buf[s, HALO:HALO + TS, :] = pa * _sigmoid(pb)
            for res in range(1, SUBLANES):
                sh_buf[s, res - 1, :, :] = a_buf[s, res:res + SH_ROWS, :]

    u_all = u_buf[...]

    def proj(c0, width):
        return jnp.dot(u_all, win_ref[:, c0:c0 + width], preferred_element_type=F32)

    def proj_ab(j):
        return proj(COL_A + j * MXU_N, MXU_N), proj(COL_B + j * MXU_N, MXU_N)

    pab_prev = proj_ab(0)
    for j in range(1, NSLICE):
        pab_next = proj_ab(j)
        glu_and_shift(j - 1, pab_prev)
        pab_prev = pab_next
    glu_and_shift(NSLICE - 1, pab_prev)

    def zero_from(v):
        bits = pltpu.bitcast(v, jnp.uint32)
        sh16 = jnp.uint32(16)
        return pltpu.bitcast(lax.shift_right_logical(lax.shift_right_logical(bits, sh16), sh16), F32)

    dyn_zero = jnp.minimum(t, 0)
    nrep = CRB // SUBLANES

    def conv_strip(s_static, token):
        s = s_static + dyn_zero
        wks = None
        for bi, r in enumerate(range(0, TS, CRB)):
            z = zero_from(token)
            cb8 = jnp.broadcast_to(cb_ref[s], (SUBLANES, LANES)) + z
            acc = jnp.concatenate([cb8] * nrep, axis=0)
            if bi % TAP_GATE_EVERY == 0:
                wks = [jnp.concatenate(
                    [jnp.broadcast_to(cw_ref[s, k:k + 1, :], (SUBLANES, LANES)) + z] * nrep, axis=0)
                    for k in range(CONV_W)]
            for res in range(SUBLANES):
                taps = [(k, off // SUBLANES) for k, off in enumerate(TAP_OFF)
                        if off % SUBLANES == res]
                span = CRB + SUBLANES * max(q for _, q in taps)
                if res == 0:
                    win = a_buf[s, r:r + span, :]
                else:
                    win = sh_buf[s, res - 1, r:r + span, :]
                for k, q in taps:
                    acc = acc + win[q * SUBLANES:q * SUBLANES + CRB, :] * wks[k]
            conv_buf[s, r:r + CRB, :] = acc
            token = acc[0:SUBLANES, :]
        return token

    token = x_ref[0, 0:SUBLANES, 0:LANES]
    for g in range(NGROUP_LOOP):
        prest_buf[g] = proj(GROUP_COL[g], GROUP_W)
        for e in range(STRIPS_PER_GROUP):
            token = conv_strip(g * STRIPS_PER_GROUP + e, token)
    prest_buf[G_RG] = proj(GROUP_COL[G_RG], GROUP_W)
    a_buf[:, 0:HALO, :] = a_buf[:, TS:TS + HALO, :]

    lng = lng_ref[...]
    lnb = lnb_ref[...]
    for rs in row_blocks:
        cv = jnp.concatenate([conv_buf[s, rs, :] for s in range(NSTRIP)], axis=-1)
        mu = jnp.mean(cv, axis=-1, keepdims=True)
        cen = cv - mu
        var = jnp.mean(cen * cen, axis=-1, keepdims=True)
        ln = cen * lax.rsqrt(var + EPS) * lng + lnb
        c_buf[rs, :] = _silu(ln).astype(BF16)

    pwo = jnp.dot(c_buf[...], pw_ref[...], preferred_element_type=F32)
    for rs in row_blocks:
        ycat_buf[rs, 0:D_CONV] = (pwo[rs, :] * _silu(prest_buf[G_GATE, rs, :])).astype(BF16)

    half = DK // 2
    kscale = DK ** -0.5
    for rs in row_blocks:
        cs = cos_ref[rs, :]
        sn = sin_ref[rs, :]
        for h in range(HEADS):
            lo = slice(h * DK, h * DK + half)
            hi = slice(h * DK + half, (h + 1) * DK)
            q1 = prest_buf[G_Q, rs, lo]
            q2 = prest_buf[G_Q, rs, hi]
            q_buf[rs, lo] = (q1 * cs - q2 * sn).astype(BF16)
            q_buf[rs, hi] = (q1 * sn + q2 * cs).astype(BF16)
            k1 = prest_buf[G_K, rs, lo]
            k2 = prest_buf[G_K, rs, hi]
            o1 = (k1 * cs - k2 * sn) * kscale
            o2 = (k1 * sn + k2 * cs) * kscale
            k_buf[rs, lo] = o1.astype(BF16)
            k_buf[rs, hi] = o2.astype(BF16)
            kdr = kd_ref[h, rs, :]
            kdec_buf[rs, lo] = (o1 * kdr).astype(BF16)
            kdec_buf[rs, hi] = (o2 * kdr).astype(BF16)
        v_buf[rs, :] = prest_buf[G_V, rs, :].astype(BF16)

    for h in range(HEADS):
        hs = slice(h * DK, (h + 1) * DK)
        q = q_buf[:, hs]
        v = v_buf[:, hs]
        st = state[h]
        scores = lax.dot_general(q, k_buf[:, hs], (((1,), (1,)), ((), ())),
                                 preferred_element_type=F32) * mask_ref[h]
        inner = jnp.dot(scores.astype(BF16), v, preferred_element_type=F32)
        qd = qd_ref[h]
        cross = jnp.dot(q, st.astype(BF16), preferred_element_type=F32)
        upd = lax.dot_general(kdec_buf[:, hs], v, (((0,), (0,)), ((), ())),
                              preferred_element_type=F32)
        state[h] = st * cd_ref[h] + upd
        o = inner + cross * jnp.concatenate([qd, qd], axis=-1)
        mu = jnp.mean(o, axis=-1, keepdims=True)
        cen = o - mu
        var = jnp.mean(cen * cen, axis=-1, keepdims=True)
        gn = cen * lax.rsqrt(var + EPS) * gng_ref[:, hs] + gnb_ref[:, hs]
        rg = prest_buf[G_RG, :, hs]
        ycat_buf[:, D_CONV + h * DV:D_CONV + (h + 1) * DV] = (gn * _silu(rg)).astype(BF16)

    yo = jnp.dot(ycat_buf[...], wout_ref[...], preferred_element_type=F32)
    fg = fg_ref[...]
    for rs in row_blocks:
        hres = x_ref[0, rs, :] + gate * yo[rs, :]
        ms = jnp.mean(hres * hres, axis=-1, keepdims=True)
        out_ref[0, rs, :] = hres * lax.rsqrt(ms + EPS) * fg


def _retention_tables():
    log_g = np.log(1.0 - np.exp2(-5.0 - np.arange(HEADS, dtype=np.float64)))
    idx = np.arange(TS, dtype=np.float64)
    diff = idx[:, None] - idx[None, :]
    mask = np.where(diff >= 0, np.exp(log_g[:, None, None] * np.maximum(diff, 0.0)[None]), 0.0)
    qd = np.exp(log_g[:, None] * (idx + 1.0)[None, :])
    kd = np.exp(log_g[:, None] * (TS - 1.0 - idx)[None, :])
    cd = np.exp(log_g * TS)
    qd = np.broadcast_to(qd[:, :, None], (HEADS, TS, DV // 2))
    kd = np.broadcast_to(kd[:, :, None], (HEADS, TS, DK // 2))
    cd = np.broadcast_to(cd[:, None, None], (HEADS, 1, DV))
    return tuple(jnp.asarray(np.ascontiguousarray(a), dtype=F32) for a in (mask, qd, kd, cd))


def _rope_tables(seq):
    inv_freq = 1.0 / (ROPE_BASE ** np.linspace(0.0, 1.0, DK // 2, dtype=np.float64))
    theta = np.arange(seq, dtype=np.float64)[:, None] * inv_freq[None, :]
    return jnp.asarray(np.cos(theta), dtype=F32), jnp.asarray(np.sin(theta), dtype=F32)


def _const_spec(shape):
    nd = len(shape)
    return pl.BlockSpec(shape, lambda b, t: (0,) * nd, pipeline_mode=pl.Buffered(1))


def kernel(x, c, ada_w, ada_b, norm_g, w_in, conv_w, conv_b, conv_ln_g, conv_ln_b,
           conv_pw, ret_gn_g, ret_gn_b, w_out, final_g):
    bsz, seq, _ = x.shape
    assert seq % TS == 0 and TS % RB == 0 and TS % CRB == 0

    mod = _adaln_mod(c, ada_w[0], ada_b[0]).reshape(bsz, 3, D_MODEL)
    cos, sin = _rope_tables(seq)
    mask, qd, kd, cd = _retention_tables()
    cw = conv_w[0].reshape(CONV_W, NSTRIP, LANES).transpose(1, 0, 2)
    cb = conv_b[0].reshape(NSTRIP, 1, LANES)
    row = lambda a: a.reshape(1, -1).astype(F32)

    operands = [
        (x, pl.BlockSpec((1, TS, D_MODEL), lambda b, t: (b, t, 0))),
        (mod, pl.BlockSpec((1, 3, D_MODEL), lambda b, t: (b, 0, 0))),
        (row(norm_g[0]), _const_spec((1, D_MODEL))),
        (w_in[0].reshape(D_MODEL // TS, TS, N_IN), pl.BlockSpec(memory_space=pl.ANY)),
        (cw, _const_spec((NSTRIP, CONV_W, LANES))),
        (cb, _const_spec((NSTRIP, 1, LANES))),
        (row(conv_ln_g[0]), _const_spec((1, D_CONV))),
        (row(conv_ln_b[0]), _const_spec((1, D_CONV))),
        (conv_pw[0].reshape(D_CONV // TS, TS, D_CONV), pl.BlockSpec(memory_space=pl.ANY)),
        (row(ret_gn_g[0]), _const_spec((1, D_RET))),
        (row(ret_gn_b[0]), _const_spec((1, D_RET))),
        (w_out[0].reshape((D_CONV + D_RET) // TS, TS, D_MODEL), pl.BlockSpec(memory_space=pl.ANY)),
        (row(final_g), _const_spec((1, D_MODEL))),
        (cos, pl.BlockSpec((TS, DK // 2), lambda b, t: (t, 0))),
        (sin, pl.BlockSpec((TS, DK // 2), lambda b, t: (t, 0))),
        (mask, _const_spec((HEADS, TS, TS))),
        (qd, _const_spec((HEADS, TS, DV // 2))),
        (kd, _const_spec((HEADS, TS, DK // 2))),
        (cd, _const_spec((HEADS, 1, DV))),
    ]
    arrays = [a for a, _ in operands]
    in_specs = [s for _, s in operands]

    scratch = [
        pltpu.VMEM((TS, D_MODEL), BF16),
        pltpu.VMEM((NSTRIP, TS + HALO, LANES), F32),
        pltpu.VMEM((NSTRIP, SUBLANES - 1, SH_ROWS, LANES), F32),
        pltpu.VMEM((NGROUP, TS, GROUP_W), F32),
        pltpu.VMEM((NSTRIP, TS, LANES), F32),
        pltpu.VMEM((TS, D_CONV), BF16),
        pltpu.VMEM((TS, HEADS * DK), BF16),
        pltpu.VMEM((TS, HEADS * DK), BF16),
        pltpu.VMEM((TS, HEADS * DK), BF16),
        pltpu.VMEM((TS, D_RET), BF16),
        pltpu.VMEM((TS, D_CONV + D_RET), BF16),
        pltpu.VMEM((HEADS, DK, DV), F32),
        pltpu.VMEM((D_MODEL, N_IN), BF16),
        pltpu.VMEM((D_CONV, D_CONV), BF16),
        pltpu.VMEM((D_CONV + D_RET, D_MODEL), BF16),
        pltpu.VMEM((STAGE_PLANES, TS, GROUP_W), F32),
        pltpu.SemaphoreType.DMA((NGROUP + STAGE_PLANES,)),
    ]

    return pl.pallas_call(
        _layer_kernel,
        out_shape=jax.ShapeDtypeStruct((bsz, seq, D_MODEL), x.dtype),
        grid=(bsz, seq // TS),
        in_specs=in_specs,
        out_specs=pl.BlockSpec((1, TS, D_MODEL), lambda b, t: (b, t, 0)),
        scratch_shapes=scratch,
        compiler_params=pltpu.CompilerParams(
            dimension_semantics=("arbitrary", "arbitrary"),
            vmem_limit_bytes=VMEM_LIMIT),
        name="hybrid_layer",
    )(*arrays)
```

```python
import numpy as np
import jax
import jax.numpy as jnp
from jax import lax
from jax.experimental import pallas as pl
from jax.experimental.pallas import tpu as pltpu

D_MODEL = 1024
D_CONV = 1024
D_RET = 1024
HEADS = 4
DK = 256
DV = 256
CONV_W = 31
ROPE_BASE = 10000.0
EPS = 1e-6
N_IN = 3 * D_CONV + 2 * HEADS * DK + 2 * D_RET

SUBLANES = 8
LANES = 128
MXU_N = 256
TS = 256
HALO = 32
RB = 16
CRB = 32
TAP_GATE_EVERY = 2
MOD_TN = 1024
STAGE_PLANES = 6
STAGE_PAIR = 2
VMEM_LIMIT = 56 * 1024 * 1024

NSLICE = D_CONV // MXU_N
GROUP_W = D_CONV
G_Q, G_K, G_V, G_GATE, G_RG = range(5)
NGROUP = 5
COL_A, COL_B = 0, D_CONV
GROUP_COL = {G_GATE: 2 * D_CONV, G_Q: 3 * D_CONV, G_K: 4 * D_CONV, G_V: 5 * D_CONV,
             G_RG: 6 * D_CONV}
NGROUP_LOOP = 4
NSTRIP = D_CONV // LANES
STRIPS_PER_SLICE = NSTRIP // NSLICE
STRIPS_PER_GROUP = NSTRIP // NGROUP_LOOP
assert GROUP_W == HEADS * DK == D_RET

TAP_OFF = [k + HALO - (CONV_W - 1) for k in range(CONV_W)]
SH_ROWS = TS + HALO - SUBLANES

NEG_LOG2E = -1.4426950408889634

F32 = jnp.float32
BF16 = jnp.bfloat16


def _sigmoid(v):
    return 1.0 / (1.0 + jnp.exp2(v * NEG_LOG2E))


def _silu(v):
    return v * _sigmoid(v)


def _mod_kernel(c_ref, w_ref, b_ref, o_ref):
    c_act = _silu(c_ref[...])
    o_ref[...] = jnp.dot(c_act, w_ref[...], preferred_element_type=F32) + b_ref[...]


def _adaln_mod(c, ada_w, ada_b):
    bsz = c.shape[0]
    n = ada_w.shape[1]
    return pl.pallas_call(
        _mod_kernel,
        out_shape=jax.ShapeDtypeStruct((bsz, n), F32),
        grid=(n // MOD_TN,),
        in_specs=[
            pl.BlockSpec((bsz, D_MODEL), lambda j: (0, 0)),
            pl.BlockSpec((D_MODEL, MOD_TN), lambda j: (0, j)),
            pl.BlockSpec((1, MOD_TN), lambda j: (0, j)),
        ],
        out_specs=pl.BlockSpec((bsz, MOD_TN), lambda j: (0, j)),
        name="adaln_mod",
    )(c, ada_w, ada_b.reshape(1, n))


def _layer_kernel(x_ref, xn_ref, mod_ref, ng_ref, win_hbm, cw_ref, cb_ref, lng_ref, lnb_ref,
                  pw_hbm, gng_ref, gnb_ref, wout_hbm, fg_ref, cos_ref, sin_ref,
                  mask_ref, qd_ref, kd_ref, cd_ref,
                  out_ref,
                  u_buf, a_buf, sh_buf, prest_buf, conv_buf, c_buf,
                  q_buf, kdec_buf, k_buf, v_buf, ycat_buf, state,
                  win_ref, pw_ref, wout_ref, stage_buf, wsem):
    t = pl.program_id(1)
    row_blocks = [slice(i * RB, (i + 1) * RB) for i in range(TS // RB)]

    @pl.when((t == 0) & (pl.program_id(0) == 0))
    def _load_weights():
        blocks = [(src, dst, rb, c0)
                  for src, dst in ((win_hbm, win_ref), (pw_hbm, pw_ref), (wout_hbm, wout_ref))
                  for rb in range(0, src.shape[0], STAGE_PAIR)
                  for c0 in range(0, src.shape[2], GROUP_W)]
        slots = ([prest_buf.at[p:p + STAGE_PAIR]
                  for p in range(0, NGROUP - STAGE_PAIR + 1, STAGE_PAIR)]
                 + [stage_buf.at[p:p + STAGE_PAIR]
                    for p in range(0, STAGE_PLANES - STAGE_PAIR + 1, STAGE_PAIR)])
        ahead = len(slots) - 1

        def block_copy(i):
            src, _, rb, c0 = blocks[i]
            slot = i % len(slots)
            return pltpu.make_async_copy(src.at[rb:rb + STAGE_PAIR, :, c0:c0 + GROUP_W],
                                         slots[slot], wsem.at[slot])

        for i in range(ahead):
            block_copy(i).start()
        for i, (_, dst, rb, c0) in enumerate(blocks):
            block_copy(i).wait()
            if i + ahead < len(blocks):
                block_copy(i + ahead).start()
            for j in range(STAGE_PAIR):
                r0 = (rb + j) * TS
                dst[r0:r0 + TS, c0:c0 + GROUP_W] = slots[i % len(slots)][j].astype(BF16)

    @pl.when(t == 0)
    def _reset():
        state[...] = jnp.zeros_like(state)
        a_buf[:, 0:HALO, :] = jnp.zeros((NSTRIP, HALO, LANES), F32)

    shift = mod_ref[0, 0:1, :]
    gain = ng_ref[...] * (1.0 + mod_ref[0, 1:2, :])
    gate = mod_ref[0, 2:3, :]

    slot = lax.rem(t, 2)

    def norm_into(src_ref, plane):
        for rs in row_blocks:
            xb = src_ref[0, rs, :]
            ms = jnp.mean(xb * xb, axis=-1, keepdims=True)
            u_buf[plane, rs, :] = (xb * lax.rsqrt(ms + EPS) * gain + shift).astype(BF16)

    @pl.when(t == 0)
    def _first_norm():
        norm_into(x_ref, 0)

    def glu_and_shift(j, pab):
        pa_all, pb_all = pab
        for e in range(STRIPS_PER_SLICE):
            s = j * STRIPS_PER_SLICE + e
            pa = pa_all[:, e * LANES:(e + 1) * LANES]
            pb = pb_all[:, e * LANES:(e + 1) * LANES]
            a_buf[s, HALO:HALO + TS, :] = pa * _sigmoid(pb)
            for res in range(1, SUBLANES):
                sh_buf[s, res - 1, :, :] = a_buf[s, res:res + SH_ROWS, :]

    u_all = u_buf[slot]

    def proj(c0, width):
        return jnp.dot(u_all, win_ref[:, c0:c0 + width], preferred_element_type=F32)

    def proj_ab(j):
        return proj(COL_A + j * MXU_N, MXU_N), proj(COL_B + j * MXU_N, MXU_N)

    pab_prev = proj_ab(0)
    for j in range(1, NSLICE):
        pab_next = proj_ab(j)
        glu_and_shift(j - 1, pab_prev)
        pab_prev = pab_next
    glu_and_shift(NSLICE - 1, pab_prev)
    norm_into(xn_ref, 1 - slot)

    def zero_from(v):
        bits = pltpu.bitcast(v, jnp.uint32)
        sh16 = jnp.uint32(16)
        return pltpu.bitcast(lax.shift_right_logical(lax.shift_right_logical(bits, sh16), sh16), F32)

    dyn_zero = jnp.minimum(t, 0)
    nrep = CRB // SUBLANES

    def conv_strip(s_static, token):
        s = s_static + dyn_zero
        wks = None
        for bi, r in enumerate(range(0, TS, CRB)):
            z = zero_from(token)
            cb8 = jnp.broadcast_to(cb_ref[s], (SUBLANES, LANES)) + z
            acc = jnp.concatenate([cb8] * nrep, axis=0)
            if bi % TAP_GATE_EVERY == 0:
                wks = [jnp.concatenate(
                    [jnp.broadcast_to(cw_ref[s, k:k + 1, :], (SUBLANES, LANES)) + z] * nrep, axis=0)
                    for k in range(CONV_W)]
            for res in range(SUBLANES):
                taps = [(k, off // SUBLANES) for k, off in enumerate(TAP_OFF)
                        if off % SUBLANES == res]
                span = CRB + SUBLANES * max(q for _, q in taps)
                if res == 0:
                    win = a_buf[s, r:r + span, :]
                else:
                    win = sh_buf[s, res - 1, r:r + span, :]
                for k, q in taps:
                    acc = acc + win[q * SUBLANES:q * SUBLANES + CRB, :] * wks[k]
            conv_buf[s, r:r + CRB, :] = acc
            token = acc[0:SUBLANES, :]
        return token

    token = x_ref[0, 0:SUBLANES, 0:LANES]
    for g in range(NGROUP_LOOP):
        prest_buf[g] = proj(GROUP_COL[g], GROUP_W)
        for e in range(STRIPS_PER_GROUP):
            token = conv_strip(g * STRIPS_PER_GROUP + e, token)
    prest_buf[G_RG] = proj(GROUP_COL[G_RG], GROUP_W)
    a_buf[:, 0:HALO, :] = a_buf[:, TS:TS + HALO, :]

    lng = lng_ref[...]
    lnb = lnb_ref[...]
    for rs in row_blocks:
        cv = jnp.concatenate([conv_buf[s, rs, :] for s in range(NSTRIP)], axis=-1)
        mu = jnp.mean(cv, axis=-1, keepdims=True)
        cen = cv - mu
        var = jnp.mean(cen * cen, axis=-1, keepdims=True)
        ln = cen * lax.rsqrt(var + EPS) * lng + lnb
        c_buf[rs, :] = _silu(ln).astype(BF16)

    pwo = jnp.dot(c_buf[...], pw_ref[...], preferred_element_type=F32)
    for rs in row_blocks:
        ycat_buf[rs, 0:D_CONV] = (pwo[rs, :] * _silu(prest_buf[G_GATE, rs, :])).astype(BF16)

    half = DK // 2
    kscale = DK ** -0.5
    for rs in row_blocks:
        cs = cos_ref[rs, :]
        sn = sin_ref[rs, :]
        for h in range(HEADS):
            lo = slice(h * DK, h * DK + half)
            hi = slice(h * DK + half, (h + 1) * DK)
            q1 = prest_buf[G_Q, rs, lo]
            q2 = prest_buf[G_Q, rs, hi]
            q_buf[rs, lo] = (q1 * cs - q2 * sn).astype(BF16)
            q_buf[rs, hi] = (q1 * sn + q2 * cs).astype(BF16)
            k1 = prest_buf[G_K, rs, lo]
            k2 = prest_buf[G_K, rs, hi]
            o1 = (k1 * cs - k2 * sn) * kscale
            o2 = (k1 * sn + k2 * cs) * kscale
            k_buf[rs, lo] = o1.astype(BF16)
            k_buf[rs, hi] = o2.astype(BF16)
            kdr = kd_ref[h, rs, :]
            kdec_buf[rs, lo] = (o1 * kdr).astype(BF16)
            kdec_buf[rs, hi] = (o2 * kdr).astype(BF16)
        v_buf[rs, :] = prest_buf[G_V, rs, :].astype(BF16)

    for h in range(HEADS):
        hs = slice(h * DK, (h + 1) * DK)
        q = q_buf[:, hs]
        v = v_buf[:, hs]
        st = state[h]
        scores = lax.dot_general(q, k_buf[:, hs], (((1,), (1,)), ((), ())),
                                 preferred_element_type=F32) * mask_ref[h]
        inner = jnp.dot(scores.astype(BF16), v, preferred_element_type=F32)
        qd = qd_ref[h]
        cross = jnp.dot(q, st.astype(BF16), preferred_element_type=F32)
        upd = lax.dot_general(kdec_buf[:, hs], v, (((0,), (0,)), ((), ())),
                              preferred_element_type=F32)
        state[h] = st * cd_ref[h] + upd
        o = inner + cross * jnp.concatenate([qd, qd], axis=-1)
        mu = jnp.mean(o, axis=-1, keepdims=True)
        cen = o - mu
        var = jnp.mean(cen * cen, axis=-1, keepdims=True)
        gn = cen * lax.rsqrt(var + EPS) * gng_ref[:, hs] + gnb_ref[:, hs]
        rg = prest_buf[G_RG, :, hs]
        ycat_buf[:, D_CONV + h * DV:D_CONV + (h + 1) * DV] = (gn * _silu(rg)).astype(BF16)

    yo = jnp.dot(ycat_buf[...], wout_ref[...], preferred_element_type=F32)
    fg = fg_ref[...]
    for rs in row_blocks:
        hres = x_ref[0, rs, :] + gate * yo[rs, :]
        ms = jnp.mean(hres * hres, axis=-1, keepdims=True)
        out_ref[0, rs, :] = hres * lax.rsqrt(ms + EPS) * fg


def _retention_tables():
    log_g = np.log(1.0 - np.exp2(-5.0 - np.arange(HEADS, dtype=np.float64)))
    idx = np.arange(TS, dtype=np.float64)
    diff = idx[:, None] - idx[None, :]
    mask = np.where(diff >= 0, np.exp(log_g[:, None, None] * np.maximum(diff, 0.0)[None]), 0.0)
    qd = np.exp(log_g[:, None] * (idx + 1.0)[None, :])
    kd = np.exp(log_g[:, None] * (TS - 1.0 - idx)[None, :])
    cd = np.exp(log_g * TS)
    qd = np.broadcast_to(qd[:, :, None], (HEADS, TS, DV // 2))
    kd = np.broadcast_to(kd[:, :, None], (HEADS, TS, DK // 2))
    cd = np.broadcast_to(cd[:, None, None], (HEADS, 1, DV))
    return tuple(jnp.asarray(np.ascontiguousarray(a), dtype=F32) for a in (mask, qd, kd, cd))


def _rope_tables(seq):
    inv_freq = 1.0 / (ROPE_BASE ** np.linspace(0.0, 1.0, DK // 2, dtype=np.float64))
    theta = np.arange(seq, dtype=np.float64)[:, None] * inv_freq[None, :]
    return jnp.asarray(np.cos(theta), dtype=F32), jnp.asarray(np.sin(theta), dtype=F32)


def _const_spec(shape):
    nd = len(shape)
    return pl.BlockSpec(shape, lambda b, t: (0,) * nd, pipeline_mode=pl.Buffered(1))


def kernel(x, c, ada_w, ada_b, norm_g, w_in, conv_w, conv_b, conv_ln_g, conv_ln_b,
           conv_pw, ret_gn_g, ret_gn_b, w_out, final_g):
    bsz, seq, _ = x.shape
    assert seq % TS == 0 and TS % RB == 0 and TS % CRB == 0

    mod = _adaln_mod(c, ada_w[0], ada_b[0]).reshape(bsz, 3, D_MODEL)
    cos, sin = _rope_tables(seq)
    mask, qd, kd, cd = _retention_tables()
    cw = conv_w[0].reshape(CONV_W, NSTRIP, LANES).transpose(1, 0, 2)
    cb = conv_b[0].reshape(NSTRIP, 1, LANES)
    row = lambda a: a.reshape(1, -1).astype(F32)

    operands = [
        (x, pl.BlockSpec((1, TS, D_MODEL), lambda b, t: (b, t, 0))),
        (x, pl.BlockSpec((1, TS, D_MODEL),
                         lambda b, t: (b, jnp.minimum(t + 1, seq // TS - 1), 0))),
        (mod, pl.BlockSpec((1, 3, D_MODEL), lambda b, t: (b, 0, 0))),
        (row(norm_g[0]), _const_spec((1, D_MODEL))),
        (w_in[0].reshape(D_MODEL // TS, TS, N_IN), pl.BlockSpec(memory_space=pl.ANY)),
        (cw, _const_spec((NSTRIP, CONV_W, LANES))),
        (cb, _const_spec((NSTRIP, 1, LANES))),
        (row(conv_ln_g[0]), _const_spec((1, D_CONV))),
        (row(conv_ln_b[0]), _const_spec((1, D_CONV))),
        (conv_pw[0].reshape(D_CONV // TS, TS, D_CONV), pl.BlockSpec(memory_space=pl.ANY)),
        (row(ret_gn_g[0]), _const_spec((1, D_RET))),
        (row(ret_gn_b[0]), _const_spec((1, D_RET))),
        (w_out[0].reshape((D_CONV + D_RET) // TS, TS, D_MODEL), pl.BlockSpec(memory_space=pl.ANY)),
        (row(final_g), _const_spec((1, D_MODEL))),
        (cos, pl.BlockSpec((TS, DK // 2), lambda b, t: (t, 0))),
        (sin, pl.BlockSpec((TS, DK // 2), lambda b, t: (t, 0))),
        (mask, _const_spec((HEADS, TS, TS))),
        (qd, _const_spec((HEADS, TS, DV // 2))),
        (kd, _const_spec((HEADS, TS, DK // 2))),
        (cd, _const_spec((HEADS, 1, DV))),
    ]
    arrays = [a for a, _ in operands]
    in_specs = [s for _, s in operands]

    scratch = [
        pltpu.VMEM((2, TS, D_MODEL), BF16),
        pltpu.VMEM((NSTRIP, TS + HALO, LANES), F32),
        pltpu.VMEM((NSTRIP, SUBLANES - 1, SH_ROWS, LANES), F32),
        pltpu.VMEM((NGROUP, TS, GROUP_W), F32),
        pltpu.VMEM((NSTRIP, TS, LANES), F32),
        pltpu.VMEM((TS, D_CONV), BF16),
        pltpu.VMEM((TS, HEADS * DK), BF16),
        pltpu.VMEM((TS, HEADS * DK), BF16),
        pltpu.VMEM((TS, HEADS * DK), BF16),
        pltpu.VMEM((TS, D_RET), BF16),
        pltpu.VMEM((TS, D_CONV + D_RET), BF16),
        pltpu.VMEM((HEADS, DK, DV), F32),
        pltpu.VMEM((D_MODEL, N_IN), BF16),
        pltpu.VMEM((D_CONV, D_CONV), BF16),
        pltpu.VMEM((D_CONV + D_RET, D_MODEL), BF16),
        pltpu.VMEM((STAGE_PLANES, TS, GROUP_W), F32),
        pltpu.SemaphoreType.DMA((NGROUP + STAGE_PLANES,)),
    ]

    return pl.pallas_call(
        _layer_kernel,
        out_shape=jax.ShapeDtypeStruct((bsz, seq, D_MODEL), x.dtype),
        grid=(bsz, seq // TS),
        in_specs=in_specs,
        out_specs=pl.BlockSpec((1, TS, D_MODEL), lambda b, t: (b, t, 0)),
        scratch_shapes=scratch,
        compiler_params=pltpu.CompilerParams(
            dimension_semantics=("arbitrary", "arbitrary"),
            vmem_limit_bytes=VMEM_LIMIT),
        name="hybrid_layer",
    )(*arrays)
```

```python
import numpy as np
import jax
import jax.numpy as jnp
from jax import lax
from jax.experimental import pallas as pl
from jax.experimental.pallas import tpu as pltpu

D_MODEL = 1024
D_CONV = 1024
D_RET = 1024
HEADS = 4
DK = 256
DV = 256
CONV_W = 31
ROPE_BASE = 10000.0
EPS = 1e-6
N_IN = 3 * D_CONV + 2 * HEADS * DK + 2 * D_RET

SUBLANES = 8
LANES = 128
MXU_N = 256
TS = 256
HALO = 32
RB = 16
CRB = 32
TAP_GATE_EVERY = 2
MOD_TN = 1024
STAGE_PLANES = 6
STAGE_PAIR = 2
VMEM_LIMIT = 56 * 1024 * 1024

NSLICE = D_CONV // MXU_N
GROUP_W = D_CONV
G_Q, G_K, G_V, G_GATE, G_RG = range(5)
NGROUP = 5
COL_A, COL_B = 0, D_CONV
GROUP_COL = {G_GATE: 2 * D_CONV, G_Q: 3 * D_CONV, G_K: 4 * D_CONV, G_V: 5 * D_CONV,
             G_RG: 6 * D_CONV}
NGROUP_LOOP = 4
NSTRIP = D_CONV // LANES
STRIPS_PER_SLICE = NSTRIP // NSLICE
STRIPS_PER_GROUP = NSTRIP // NGROUP_LOOP
assert GROUP_W == HEADS * DK == D_RET

TAP_OFF = [k + HALO - (CONV_W - 1) for k in range(CONV_W)]
SH_ROWS = TS + HALO - SUBLANES

NEG_LOG2E = -1.4426950408889634

F32 = jnp.float32
BF16 = jnp.bfloat16


def _sigmoid(v):
    return 1.0 / (1.0 + jnp.exp2(v * NEG_LOG2E))


def _silu(v):
    return v * _sigmoid(v)


def _mod_kernel(c_ref, w_ref, b_ref, o_ref):
    c_act = _silu(c_ref[...])
    o_ref[...] = jnp.dot(c_act, w_ref[...], preferred_element_type=F32) + b_ref[...]


def _adaln_mod(c, ada_w, ada_b):
    bsz = c.shape[0]
    n = ada_w.shape[1]
    return pl.pallas_call(
        _mod_kernel,
        out_shape=jax.ShapeDtypeStruct((bsz, n), F32),
        grid=(n // MOD_TN,),
        in_specs=[
            pl.BlockSpec((bsz, D_MODEL), lambda j: (0, 0)),
            pl.BlockSpec((D_MODEL, MOD_TN), lambda j: (0, j)),
            pl.BlockSpec((1, MOD_TN), lambda j: (0, j)),
        ],
        out_specs=pl.BlockSpec((bsz, MOD_TN), lambda j: (0, j)),
        name="adaln_mod",
    )(c, ada_w, ada_b.reshape(1, n))


def _layer_kernel(x_ref, xn_ref, mod_ref, ng_ref, win_hbm, cw_ref, cb_ref, lng_ref, lnb_ref,
                  pw_hbm, gng_ref, gnb_ref, wout_hbm, fg_ref, cos_ref, sin_ref,
                  mask_ref, qd_ref, kd_ref, cd_ref,
                  out_ref,
                  u_buf, a_buf, sh_buf, prest_buf, conv_buf, c_buf,
                  q_buf, kdec_buf, k_buf, v_buf, ycat_buf, state,
                  win_ref, pw_ref, wout_ref, stage_buf, wsem):
    t = pl.program_id(1)
    row_blocks = [slice(i * RB, (i + 1) * RB) for i in range(TS // RB)]

    @pl.when((t == 0) & (pl.program_id(0) == 0))
    def _load_weights():
        blocks = [(src, dst, rb, c0)
                  for src, dst in ((win_hbm, win_ref), (pw_hbm, pw_ref), (wout_hbm, wout_ref))
                  for rb in range(0, src.shape[0], STAGE_PAIR)
                  for c0 in range(0, src.shape[2], GROUP_W)]
        slots = ([prest_buf.at[p:p + STAGE_PAIR]
                  for p in range(0, NGROUP - STAGE_PAIR + 1, STAGE_PAIR)]
                 + [stage_buf.at[p:p + STAGE_PAIR]
                    for p in range(0, STAGE_PLANES - STAGE_PAIR + 1, STAGE_PAIR)])
        ahead = len(slots) - 1

        def block_copy(i):
            src, _, rb, c0 = blocks[i]
            slot = i % len(slots)
            return pltpu.make_async_copy(src.at[rb:rb + STAGE_PAIR, :, c0:c0 + GROUP_W],
                                         slots[slot], wsem.at[slot])

        for i in range(ahead):
            block_copy(i).start()
        for i, (_, dst, rb, c0) in enumerate(blocks):
            block_copy(i).wait()
            if i + ahead < len(blocks):
                block_copy(i + ahead).start()
            for j in range(STAGE_PAIR):
                r0 = (rb + j) * TS
                dst[r0:r0 + TS, c0:c0 + GROUP_W] = slots[i % len(slots)][j].astype(BF16)

    @pl.when(t == 0)
    def _reset():
        state[...] = jnp.zeros_like(state)
        a_buf[:, 0:HALO, :] = jnp.zeros((NSTRIP, HALO, LANES), F32)

    shift = mod_ref[0, 0:1, :]
    gain = ng_ref[...] * (1.0 + mod_ref[0, 1:2, :])
    gate = mod_ref[0, 2:3, :]

    slot = lax.rem(t, 2)

    def norm_into(src_ref, plane):
        for rs in row_blocks:
            xb = src_ref[0, rs, :]
            ms = jnp.mean(xb * xb, axis=-1, keepdims=True)
            u_buf[plane, rs, :] = (xb * lax.rsqrt(ms + EPS) * gain + shift).astype(BF16)

    @pl.when(t == 0)
    def _first_norm():
        norm_into(x_ref, 0)

    def glu_and_shift(j, pab):
        pa_all, pb_all = pab
        for e in range(STRIPS_PER_SLICE):
            s = j * STRIPS_PER_SLICE + e
            pa = pa_all[:, e * LANES:(e + 1) * LANES]
            pb = pb_all[:, e * LANES:(e + 1) * LANES]
            a_buf[s, HALO:HALO + TS, :] = pa * _sigmoid(pb)
            for res in range(1, SUBLANES):
                sh_buf[s, res - 1, :, :] = a_buf[s, res:res + SH_ROWS, :]

    u_all = u_buf[slot]

    def proj(c0, width):
        return jnp.dot(u_all, win_ref[:, c0:c0 + width], preferred_element_type=F32)

    def proj_ab(j):
        return proj(COL_A + j * MXU_N, MXU_N), proj(COL_B + j * MXU_N, MXU_N)

    pab_prev = proj_ab(0)
    for j in range(1, NSLICE):
        pab_next = proj_ab(j)
        glu_and_shift(j - 1, pab_prev)
        pab_prev = pab_next
    glu_and_shift(NSLICE - 1, pab_prev)
    norm_into(xn_ref, 1 - slot)

    def zero_from(v):
        bits = pltpu.bitcast(v, jnp.uint32)
        sh16 = jnp.uint32(16)
        return pltpu.bitcast(lax.shift_right_logical(lax.shift_right_logical(bits, sh16), sh16), F32)

    dyn_zero = jnp.minimum(t, 0)
    nrep = CRB // SUBLANES

    def conv_strip(s_static, token):
        s = s_static + dyn_zero
        wks = None
        for bi, r in enumerate(range(0, TS, CRB)):
            z = zero_from(token)
            cb8 = jnp.broadcast_to(cb_ref[s], (SUBLANES, LANES)) + z
            acc = jnp.concatenate([cb8] * nrep, axis=0)
            if bi % TAP_GATE_EVERY == 0:
                wks = [jnp.concatenate(
                    [jnp.broadcast_to(cw_ref[s, k:k + 1, :], (SUBLANES, LANES)) + z] * nrep, axis=0)
                    for k in range(CONV_W)]
            for res in range(SUBLANES):
                taps = [(k, off // SUBLANES) for k, off in enumerate(TAP_OFF)
                        if off % SUBLANES == res]
                span = CRB + SUBLANES * max(q for _, q in taps)
                if res == 0:
                    win = a_buf[s, r:r + span, :]
                else:
                    win = sh_buf[s, res - 1, r:r + span, :]
                for k, q in taps:
                    acc = acc + win[q * SUBLANES:q * SUBLANES + CRB, :] * wks[k]
            conv_buf[s, r:r + CRB, :] = acc
            token = acc[0:SUBLANES, :]
        return token

    token = x_ref[0, 0:SUBLANES, 0:LANES]
    for g in range(NGROUP_LOOP):
        prest_buf[g] = proj(GROUP_COL[g], GROUP_W)
        for e in range(STRIPS_PER_GROUP):
            token = conv_strip(g * STRIPS_PER_GROUP + e, token)
    prest_buf[G_RG] = proj(GROUP_COL[G_RG], GROUP_W)
    a_buf[:, 0:HALO, :] = a_buf[:, TS:TS + HALO, :]

    lng = lng_ref[...]
    lnb = lnb_ref[...]
    for rs in row_blocks:
        cv = jnp.concatenate([conv_buf[s, rs, :] for s in range(NSTRIP)], axis=-1)
        mu = jnp.mean(cv, axis=-1, keepdims=True)
        cen = cv - mu
        var = jnp.mean(cen * cen, axis=-1, keepdims=True)
        ln = cen * lax.rsqrt(var + EPS) * lng + lnb
        c_buf[rs, :] = _silu(ln).astype(BF16)

    pwo = jnp.dot(c_buf[...], pw_ref[...], preferred_element_type=F32)
    for rs in row_blocks:
        ycat_buf[rs, 0:D_CONV] = (pwo[rs, :] * _silu(prest_buf[G_GATE, rs, :])).astype(BF16)

    half = DK // 2
    kscale = DK ** -0.5
    for rs in row_blocks:
        cs = cos_ref[rs, :]
        sn = sin_ref[rs, :]
        for h in range(HEADS):
            lo = slice(h * DK, h * DK + half)
            hi = slice(h * DK + half, (h + 1) * DK)
            q1 = prest_buf[G_Q, rs, lo]
            q2 = prest_buf[G_Q, rs, hi]
            q_buf[rs, lo] = (q1 * cs - q2 * sn).astype(BF16)
            q_buf[rs, hi] = (q1 * sn + q2 * cs).astype(BF16)
            k1 = prest_buf[G_K, rs, lo]
            k2 = prest_buf[G_K, rs, hi]
            o1 = (k1 * cs - k2 * sn) * kscale
            o2 = (k1 * sn + k2 * cs) * kscale
            k_buf[rs, lo] = o1.astype(BF16)
            k_buf[rs, hi] = o2.astype(BF16)
            kdr = kd_ref[h, rs, :]
            kdec_buf[rs, lo] = (o1 * kdr).astype(BF16)
            kdec_buf[rs, hi] = (o2 * kdr).astype(BF16)
        v_buf[rs, :] = prest_buf[G_V, rs, :].astype(BF16)

    for h in range(HEADS):
        hs = slice(h * DK, (h + 1) * DK)
        q = q_buf[:, hs]
        v = v_buf[:, hs]
        st = state[h]
        scores = lax.dot_general(q, k_buf[:, hs], (((1,), (1,)), ((), ())),
                                 preferred_element_type=F32) * mask_ref[h]
        qd = qd_ref[h]
        q_dec = (q.astype(F32) * jnp.concatenate([qd, qd], axis=-1)).astype(BF16)
        o = jnp.dot(jnp.concatenate([scores.astype(BF16), q_dec], axis=1),
                    jnp.concatenate([v, st.astype(BF16)], axis=0),
                    preferred_element_type=F32)
        upd = lax.dot_general(kdec_buf[:, hs], v, (((0,), (0,)), ((), ())),
                              preferred_element_type=F32)
        state[h] = st * cd_ref[h] + upd
        mu = jnp.mean(o, axis=-1, keepdims=True)
        cen = o - mu
        var = jnp.mean(cen * cen, axis=-1, keepdims=True)
        gn = cen * lax.rsqrt(var + EPS) * gng_ref[:, hs] + gnb_ref[:, hs]
        rg = prest_buf[G_RG, :, hs]
        ycat_buf[:, D_CONV + h * DV:D_CONV + (h + 1) * DV] = (gn * _silu(rg)).astype(BF16)

    yo = jnp.dot(ycat_buf[...], wout_ref[...], preferred_element_type=F32)
    fg = fg_ref[...]
    for rs in row_blocks:
        hres = x_ref[0, rs, :] + gate * yo[rs, :]
        ms = jnp.mean(hres * hres, axis=-1, keepdims=True)
        out_ref[0, rs, :] = hres * lax.rsqrt(ms + EPS) * fg


def _retention_tables():
    log_g = np.log(1.0 - np.exp2(-5.0 - np.arange(HEADS, dtype=np.float64)))
    idx = np.arange(TS, dtype=np.float64)
    diff = idx[:, None] - idx[None, :]
    mask = np.where(diff >= 0, np.exp(log_g[:, None, None] * np.maximum(diff, 0.0)[None]), 0.0)
    qd = np.exp(log_g[:, None] * (idx + 1.0)[None, :])
    kd = np.exp(log_g[:, None] * (TS - 1.0 - idx)[None, :])
    cd = np.exp(log_g * TS)
    qd = np.broadcast_to(qd[:, :, None], (HEADS, TS, DV // 2))
    kd = np.broadcast_to(kd[:, :, None], (HEADS, TS, DK // 2))
    cd = np.broadcast_to(cd[:, None, None], (HEADS, 1, DV))
    return tuple(jnp.asarray(np.ascontiguousarray(a), dtype=F32) for a in (mask, qd, kd, cd))


def _rope_tables(seq):
    inv_freq = 1.0 / (ROPE_BASE ** np.linspace(0.0, 1.0, DK // 2, dtype=np.float64))
    theta = np.arange(seq, dtype=np.float64)[:, None] * inv_freq[None, :]
    return jnp.asarray(np.cos(theta), dtype=F32), jnp.asarray(np.sin(theta), dtype=F32)


def _const_spec(shape):
    nd = len(shape)
    return pl.BlockSpec(shape, lambda b, t: (0,) * nd, pipeline_mode=pl.Buffered(1))


def kernel(x, c, ada_w, ada_b, norm_g, w_in, conv_w, conv_b, conv_ln_g, conv_ln_b,
           conv_pw, ret_gn_g, ret_gn_b, w_out, final_g):
    bsz, seq, _ = x.shape
    assert seq % TS == 0 and TS % RB == 0 and TS % CRB == 0

    mod = _adaln_mod(c, ada_w[0], ada_b[0]).reshape(bsz, 3, D_MODEL)
    cos, sin = _rope_tables(seq)
    mask, qd, kd, cd = _retention_tables()
    cw = conv_w[0].reshape(CONV_W, NSTRIP, LANES).transpose(1, 0, 2)
    cb = conv_b[0].reshape(NSTRIP, 1, LANES)
    row = lambda a: a.reshape(1, -1).astype(F32)

    operands = [
        (x, pl.BlockSpec((1, TS, D_MODEL), lambda b, t: (b, t, 0))),
        (x, pl.BlockSpec((1, TS, D_MODEL),
                         lambda b, t: (b, jnp.minimum(t + 1, seq // TS - 1), 0))),
        (mod, pl.BlockSpec((1, 3, D_MODEL), lambda b, t: (b, 0, 0))),
        (row(norm_g[0]), _const_spec((1, D_MODEL))),
        (w_in[0].reshape(D_MODEL // TS, TS, N_IN), pl.BlockSpec(memory_space=pl.ANY)),
        (cw, _const_spec((NSTRIP, CONV_W, LANES))),
        (cb, _const_spec((NSTRIP, 1, LANES))),
        (row(conv_ln_g[0]), _const_spec((1, D_CONV))),
        (row(conv_ln_b[0]), _const_spec((1, D_CONV))),
        (conv_pw[0].reshape(D_CONV // TS, TS, D_CONV), pl.BlockSpec(memory_space=pl.ANY)),
        (row(ret_gn_g[0]), _const_spec((1, D_RET))),
        (row(ret_gn_b[0]), _const_spec((1, D_RET))),
        (w_out[0].reshape((D_CONV + D_RET) // TS, TS, D_MODEL), pl.BlockSpec(memory_space=pl.ANY)),
        (row(final_g), _const_spec((1, D_MODEL))),
        (cos, pl.BlockSpec((TS, DK // 2), lambda b, t: (t, 0))),
        (sin, pl.BlockSpec((TS, DK // 2), lambda b, t: (t, 0))),
        (mask, _const_spec((HEADS, TS, TS))),
        (qd, _const_spec((HEADS, TS, DV // 2))),
        (kd, _const_spec((HEADS, TS, DK // 2))),
        (cd, _const_spec((HEADS, 1, DV))),
    ]
    arrays = [a for a, _ in operands]
    in_specs = [s for _, s in operands]

    scratch = [
        pltpu.VMEM((2, TS, D_MODEL), BF16),
        pltpu.VMEM((NSTRIP, TS + HALO, LANES), F32),
        pltpu.VMEM((NSTRIP, SUBLANES - 1, SH_ROWS, LANES), F32),
        pltpu.VMEM((NGROUP, TS, GROUP_W), F32),
        pltpu.VMEM((NSTRIP, TS, LANES), F32),
        pltpu.VMEM((TS, D_CONV), BF16),
        pltpu.VMEM((TS, HEADS * DK), BF16),
        pltpu.VMEM((TS, HEADS * DK), BF16),
        pltpu.VMEM((TS, HEADS * DK), BF16),
        pltpu.VMEM((TS, D_RET), BF16),
        pltpu.VMEM((TS, D_CONV + D_RET), BF16),
        pltpu.VMEM((HEADS, DK, DV), F32),
        pltpu.VMEM((D_MODEL, N_IN), BF16),
        pltpu.VMEM((D_CONV, D_CONV), BF16),
        pltpu.VMEM((D_CONV + D_RET, D_MODEL), BF16),
        pltpu.VMEM((STAGE_PLANES, TS, GROUP_W), F32),
        pltpu.SemaphoreType.DMA((NGROUP + STAGE_PLANES,)),
    ]

    return pl.pallas_call(
        _layer_kernel,
        out_shape=jax.ShapeDtypeStruct((bsz, seq, D_MODEL), x.dtype),
        grid=(bsz, seq // TS),
        in_specs=in_specs,
        out_specs=pl.BlockSpec((1, TS, D_MODEL), lambda b, t: (b, t, 0)),
        scratch_shapes=scratch,
        compiler_params=pltpu.CompilerParams(
            dimension_semantics=("arbitrary", "arbitrary"),
            vmem_limit_bytes=VMEM_LIMIT),
        name="hybrid_layer",
    )(*arrays)
```
